```python
import jax, jax.numpy as jnp
from jax import lax
import numpy as np

D_MODEL = 1024
BATCH = 2
SEQ = 8192
DEPTH = 4
DEC_BATCH = 32
DEC_SEQ = 4
PAST_LEN = 8192
PAGE_SIZE = 128

N_MIXERS = 3
N_A = (DEPTH + 2) // 3
N_B = (DEPTH + 1) // 3
N_C = DEPTH // 3
CHUNK = 128
A_GROUPS = 4
A_GROUP_DIM = D_MODEL // A_GROUPS
CONV_W = 3
N_HEADS = 16
HEAD_DIM = D_MODEL // N_HEADS
Q_BLOCK = 128
SB_BIAS_INIT = -6.0
D_FF = -(-8 * D_MODEL // (3 * 256)) * 256
PLE_DIM = 256
ALPHA = (2 * DEPTH) ** 0.25
BETA = (8 * DEPTH) ** -0.25
LN_EPS = 1e-5

kernel_name = "hybrid_chunkmlp_shortconv_stickbreak_decoder_step"


def layer_norm(x, g, b):
    xf = x.astype(jnp.float32)
    mu = jnp.mean(xf, axis=-1, keepdims=True)
    var = jnp.mean(jnp.square(xf - mu), axis=-1, keepdims=True)
    y = (xf - mu) * lax.rsqrt(var + LN_EPS) * g.astype(jnp.float32) + b.astype(jnp.float32)
    return y.astype(x.dtype)


def chunk_spatial(v, w_s, b_s):
    bsz, t, d = v.shape
    pad = (-t) % CHUNK
    vp = jnp.pad(v, ((0, 0), (0, pad), (0, 0)))
    nc = (t + pad) // CHUNK
    vg = vp.reshape(bsz, nc, CHUNK, A_GROUPS, A_GROUP_DIM)
    causal = jnp.tril(jnp.ones((CHUNK, CHUNK), dtype=bool))
    w = jnp.where(causal[None], w_s, jnp.zeros_like(w_s))
    out = jnp.einsum('gij,bcjgd->bcigd', w, vg) + b_s.T[None, None, :, :, None]
    return out.reshape(bsz, nc * CHUNK, d)[:, :t]


def chunk_mlp_mixer(x, w_in, ln_g, ln_b, w_s, b_s, w_out):
    h = jax.nn.gelu(x @ w_in)
    u, v = jnp.split(h, 2, axis=-1)
    v = layer_norm(v, ln_g, ln_b)
    return (u * chunk_spatial(v, w_s, b_s)) @ w_out, v


def short_conv_mixer(x, w_in, conv_w, w_out, conv_state):
    bsz, t, d = x.shape
    b_gate, c_gate, h = jnp.split(x @ w_in, 3, axis=-1)
    xc = c_gate * h
    if conv_state is None:
        prefix = jnp.zeros((bsz, CONV_W - 1, d), xc.dtype)
    else:
        prefix = conv_state.astype(xc.dtype)
    xp = jnp.concatenate([prefix, xc], axis=1)
    y = conv_w[0] * xp[:, 0:t]
    for j in range(1, CONV_W):
        y = y + conv_w[j] * xp[:, j:j + t]
    return (b_gate * y) @ w_out, xp[:, -(CONV_W - 1):]


def stick_breaking(q, k, v, bias, q_offset):
    bsz, tq, h, dh = q.shape
    tk = k.shape[1]
    blk = min(Q_BLOCK, tq)
    pad = (-tq) % blk
    nb = (tq + pad) // blk
    qf = jnp.pad(q.astype(jnp.float32), ((0, 0), (0, pad), (0, 0), (0, 0)))
    qb = jnp.moveaxis(qf.reshape(bsz, nb, blk, h, dh), 1, 0)
    q_pos = (q_offset + jnp.arange(tq + pad, dtype=jnp.int32)).reshape(nb, blk)
    k_pos = jnp.arange(tk, dtype=jnp.int32)
    kf = k.astype(jnp.float32)
    vf = v.astype(jnp.float32)
    bf = bias.astype(jnp.float32)[None, :, None, None]
    scale = dh ** -0.5

    def one_block(args):
        qi, pos = args
        z = jnp.einsum('bqhd,bkhd->bhqk', qi, kf) * scale + bf
        mask = (k_pos[None, :] < pos[:, None])[None, None]
        log_keep = jnp.where(mask, jax.nn.log_sigmoid(-z), 0.0)
        after = lax.cumsum(log_keep, axis=3, reverse=True) - log_keep
        att = jnp.where(mask, jnp.exp(jax.nn.log_sigmoid(z) + after), 0.0)
        return jnp.einsum('bhqk,bkhd->bqhd', att, vf)

    out = lax.map(one_block, (qb, q_pos))
    out = jnp.moveaxis(out, 0, 1).reshape(bsz, nb * blk, h, dh)[:, :tq]
    return out.astype(q.dtype)


def stick_breaking_mixer(x, w_qkv, b_beta, w_out, k_past, v_past):
    bsz, t, d = x.shape
    q, k, v = jnp.split(x @ w_qkv, 3, axis=-1)
    q = q.reshape(bsz, t, N_HEADS, HEAD_DIM)
    k = k.reshape(bsz, t, N_HEADS, HEAD_DIM)
    v = v.reshape(bsz, t, N_HEADS, HEAD_DIM)
    if k_past is None:
        k_all, v_all, offset = k, v, 0
    else:
        k_all = jnp.concatenate([k_past.astype(k.dtype), k], axis=1)
        v_all = jnp.concatenate([v_past.astype(v.dtype), v], axis=1)
        offset = k_past.shape[1]
    o = stick_breaking(q, k_all, v_all, b_beta, offset)
    return o.reshape(bsz, t, d) @ w_out, k, v


def swiglu(x, w_up, w_down):
    g, u = jnp.split(x @ w_up, 2, axis=-1)
    return (jax.nn.silu(g) * u) @ w_down


def trunk(x, p, W, conv_state, cache_k, cache_v, page_table):
    chunk_v, conv_new, k_new, v_new = [], [], [], []
    for i in range(DEPTH):
        j = i // N_MIXERS
        kind = i % N_MIXERS
        if kind == 0:
            mix, vrows = chunk_mlp_mixer(x, W['a_w_in'][j], W['a_ln_g'][j], W['a_ln_b'][j],
                                         W['a_w_s'][j], W['a_b_s'][j], W['a_w_out'][j])
            chunk_v.append(vrows)
        elif kind == 1:
            st = None if conv_state is None else conv_state[j]
            mix, cs = short_conv_mixer(x, W['b_w_in'][j], W['b_conv'][j], W['b_w_out'][j], st)
            conv_new.append(cs)
        else:
            if cache_k is None:
                kp = vp = None
            else:
                nseq, npg = page_table.shape
                past = npg * cache_k.shape[2]
                kp = cache_k[j][page_table].reshape(nseq, past, N_HEADS, HEAD_DIM)
                vp = cache_v[j][page_table].reshape(nseq, past, N_HEADS, HEAD_DIM)
            mix, kn, vn = stick_breaking_mixer(x, W['c_w_qkv'][j], W['c_b_beta'][j], W['c_w_out'][j], kp, vp)
            k_new.append(kn)
            v_new.append(vn)
        x = layer_norm(ALPHA * x + mix, W['ln_mix_g'][i], W['ln_mix_b'][i])
        x = layer_norm(ALPHA * x + swiglu(x, W['f_w_up'][i], W['f_w_down'][i]),
                       W['ln_ffn_g'][i], W['ln_ffn_b'][i])
        gate = jax.nn.sigmoid(x @ W['ple_w_gate'][i])
        x = x + gate * (p[i] @ W['ple_w_proj'][i])
    return x, jnp.stack(chunk_v), jnp.stack(conv_new), jnp.stack(k_new), jnp.stack(v_new)


def setup_inputs(seed: int = 0) -> dict:
    key = jax.random.key(seed)
    ks = jax.random.split(key, 32)
    f32 = jnp.float32
    n_pages = PAST_LEN // PAGE_SIZE
    n_used = DEC_BATCH * n_pages
    n_pool = n_used + max(1, n_used // 4)
    nrm = lambda k, shape, s: jax.random.normal(k, shape, f32) * s
    page_table = jax.random.permutation(ks[0], n_pool)[:n_used].reshape(DEC_BATCH, n_pages).astype(jnp.int32)
    return {
        "x_prompt": nrm(ks[1], (BATCH, SEQ, D_MODEL), 1.0),
        "x_sample": nrm(ks[2], (DEC_BATCH, DEC_SEQ, D_MODEL), 1.0),
        "cache_k": nrm(ks[3], (N_C, n_pool, PAGE_SIZE, N_HEADS, HEAD_DIM), 1.0),
        "cache_v": nrm(ks[4], (N_C, n_pool, PAGE_SIZE, N_HEADS, HEAD_DIM), 1.0),
        "state_conv": nrm(ks[5], (N_B, DEC_BATCH, CONV_W - 1, D_MODEL), 1.0),
        "page_table": page_table,
        "p_prompt": nrm(ks[6], (DEPTH, BATCH, SEQ, PLE_DIM), 1.0),
        "p_sample": nrm(ks[7], (DEPTH, DEC_BATCH, DEC_SEQ, PLE_DIM), 1.0),
        "a_w_in": nrm(ks[8], (N_A, D_MODEL, 2 * D_MODEL), D_MODEL ** -0.5),
        "a_ln_g": 1.0 + nrm(ks[9], (N_A, D_MODEL), 0.02),
        "a_ln_b": nrm(ks[10], (N_A, D_MODEL), 0.02),
        "a_w_s": nrm(ks[11], (N_A, A_GROUPS, CHUNK, CHUNK), CHUNK ** -0.5),
        "a_b_s": 1.0 + nrm(ks[12], (N_A, A_GROUPS, CHUNK), 0.02),
        "a_w_out": nrm(ks[13], (N_A, D_MODEL, D_MODEL), BETA * D_MODEL ** -0.5),
        "b_w_in": nrm(ks[14], (N_B, D_MODEL, 3 * D_MODEL), D_MODEL ** -0.5),
        "b_conv": nrm(ks[15], (N_B, CONV_W, D_MODEL), CONV_W ** -0.5),
        "b_w_out": nrm(ks[16], (N_B, D_MODEL, D_MODEL), BETA * D_MODEL ** -0.5),
        "c_w_qkv": nrm(ks[17], (N_C, D_MODEL, 3 * D_MODEL), D_MODEL ** -0.5),
        "c_b_beta": SB_BIAS_INIT + nrm(ks[27], (N_C, N_HEADS), 0.1),
        "c_w_out": nrm(ks[18], (N_C, D_MODEL, D_MODEL), BETA * D_MODEL ** -0.5),
        "ln_mix_g": 1.0 + nrm(ks[19], (DEPTH, D_MODEL), 0.02),
        "ln_mix_b": nrm(ks[20], (DEPTH, D_MODEL), 0.02),
        "ln_ffn_g": 1.0 + nrm(ks[21], (DEPTH, D_MODEL), 0.02),
        "ln_ffn_b": nrm(ks[22], (DEPTH, D_MODEL), 0.02),
        "f_w_up": nrm(ks[23], (DEPTH, D_MODEL, 2 * D_FF), D_MODEL ** -0.5),
        "f_w_down": nrm(ks[24], (DEPTH, D_FF, D_MODEL), BETA * D_FF ** -0.5),
        "ple_w_gate": nrm(ks[25], (DEPTH, D_MODEL, D_MODEL), D_MODEL ** -0.5),
        "ple_w_proj": nrm(ks[26], (DEPTH, PLE_DIM, D_MODEL), PLE_DIM ** -0.5),
    }


def reference(x_prompt, x_sample, cache_k, cache_v, state_conv, page_table, p_prompt, p_sample,
              a_w_in, a_ln_g, a_ln_b, a_w_s, a_b_s, a_w_out, b_w_in, b_conv, b_w_out,
              c_w_qkv, c_b_beta, c_w_out, ln_mix_g, ln_mix_b, ln_ffn_g, ln_ffn_b, f_w_up, f_w_down,
              ple_w_gate, ple_w_proj):
    W = dict(a_w_in=a_w_in, a_ln_g=a_ln_g, a_ln_b=a_ln_b, a_w_s=a_w_s, a_b_s=a_b_s, a_w_out=a_w_out,
             b_w_in=b_w_in, b_conv=b_conv, b_w_out=b_w_out, c_w_qkv=c_w_qkv, c_b_beta=c_b_beta,
             c_w_out=c_w_out, ln_mix_g=ln_mix_g, ln_mix_b=ln_mix_b, ln_ffn_g=ln_ffn_g, ln_ffn_b=ln_ffn_b,
             f_w_up=f_w_up, f_w_down=f_w_down, ple_w_gate=ple_w_gate, ple_w_proj=ple_w_proj)
    y_prompt, _, conv_prompt, k_prompt, v_prompt = trunk(x_prompt, p_prompt, W, None, None, None, None)
    y_sample, chunk_v_sample, conv_sample, k_sample, v_sample = trunk(
        x_sample, p_sample, W, state_conv, cache_k, cache_v, page_table)
    return (y_prompt, y_sample, k_prompt, v_prompt, k_sample, v_sample, conv_prompt, conv_sample, chunk_v_sample)
```

```python
import functools

import jax
import jax.numpy as jnp
from jax import lax
from jax.experimental import pallas as pl
from jax.experimental.pallas import tpu as pltpu

F32 = jnp.float32
BF16 = jnp.bfloat16

DEPTH = 4
N_MIXERS = 3
CHUNK = 128
A_GROUPS = 4
CONV_W = 3
N_HEADS = 16
ALPHA = (2 * DEPTH) ** 0.25
LN_EPS = 1e-5

LANES = 128
SUBLANES = 8
ROW_TILE = 512
ATTN_TQ = 128
ATTN_TK = 256
NEW_PAD = 16
VMEM_LIMIT = 56 * 1024 * 1024


def _params(*semantics):
    return pltpu.CompilerParams(dimension_semantics=semantics, vmem_limit_bytes=VMEM_LIMIT)


def _resident(shape):
    zeros = (0,) * len(shape)
    return pl.BlockSpec(shape, lambda *_: zeros, pipeline_mode=pl.Buffered(1))


def _rows(tm, width):
    return pl.BlockSpec((tm, width), lambda i: (i, 0))


def _dot(a, b):
    return jnp.dot(a, b, preferred_element_type=F32)


def _layer_norm(x, g, b):
    mu = jnp.mean(x, axis=-1, keepdims=True)
    xc = x - mu
    var = jnp.mean(xc * xc, axis=-1, keepdims=True)
    return xc * lax.rsqrt(var + LN_EPS) * g + b


def _sigmoid(x):
    return 1.0 / (1.0 + jnp.exp(-x))


def _chunk_mixer_body(x_ref, w_in_ref, lng_ref, lnb_ref, ws_ref, bmap_ref, w_out_ref, mg_ref, mb_ref,
                      out_ref, *rest, emit_v):
    if emit_v:
        v_out_ref, u_ref, v_ref, s_ref = rest
    else:
        u_ref, v_ref, s_ref = rest
    tm, d = x_ref.shape
    gd = d // A_GROUPS
    x = x_ref[...]
    xb = x.astype(BF16)
    u_ref[...] = jax.nn.gelu(_dot(xb, w_in_ref[:, :d]))
    v = _layer_norm(jax.nn.gelu(_dot(xb, w_in_ref[:, d:])), lng_ref[...], lnb_ref[...])
    if emit_v:
        v_out_ref[...] = v
    v_ref[...] = v.astype(BF16)
    for c in range(tm // CHUNK):
        r0 = c * CHUNK
        for g in range(A_GROUPS):
            c0 = g * gd
            s_ref[r0:r0 + CHUNK, c0:c0 + gd] = (
                _dot(ws_ref[g], v_ref[r0:r0 + CHUNK, c0:c0 + gd]) + bmap_ref[:, c0:c0 + gd])
    gated = (u_ref[...] * s_ref[...]).astype(BF16)
    mix = _dot(gated, w_out_ref[...])
    out_ref[...] = _layer_norm(ALPHA * x + mix, mg_ref[...], mb_ref[...])


def _chunk_mixer(x, w_in, ln_g, ln_b, w_s, bmap, w_out, mg, mb, *, tm, emit_v):
    t, d = x.shape
    out_shape = [jax.ShapeDtypeStruct((t, d), F32)]
    out_specs = [_rows(tm, d)]
    if emit_v:
        out_shape.append(jax.ShapeDtypeStruct((t, d), F32))
        out_specs.append(_rows(tm, d))
    res = pl.pallas_call(
        functools.partial(_chunk_mixer_body, emit_v=emit_v),
        grid=(t // tm,),
        in_specs=[_rows(tm, d), _resident(w_in.shape), _resident(ln_g.shape), _resident(ln_b.shape),
                  _resident(w_s.shape), _resident(bmap.shape), _resident(w_out.shape),
                  _resident(mg.shape), _resident(mb.shape)],
        out_specs=out_specs,
        out_shape=out_shape,
        scratch_shapes=[pltpu.VMEM((tm, d), F32), pltpu.VMEM((tm, d), BF16), pltpu.VMEM((tm, d), F32)],
        compiler_params=_params("arbitrary"),
        name="chunk_mixer",
    )(x, w_in, ln_g, ln_b, w_s, bmap, w_out, mg, mb)
    return res if emit_v else (res[0], None)


def _conv_project(x_ref, w_in_ref, buf_ref):
    tm, d = x_ref.shape
    xb = x_ref[...].astype(BF16)
    b_gate = _dot(xb, w_in_ref[:, :d])
    xc = _dot(xb, w_in_ref[:, d:2 * d]) * _dot(xb, w_in_ref[:, 2 * d:])
    buf_ref[SUBLANES:SUBLANES + tm, :] = xc
    return b_gate, xc


def _conv_finish(x_ref, b_gate, y, w_out_ref, mg_ref, mb_ref, out_ref):
    mix = _dot((b_gate * y).astype(BF16), w_out_ref[...])
    out_ref[...] = _layer_norm(ALPHA * x_ref[...] + mix, mg_ref[...], mb_ref[...])


def _conv_prompt_body(x_ref, w_in_ref, cw_ref, w_out_ref, mg_ref, mb_ref, out_ref, tail_ref, buf_ref,
                      *, tiles_per_seq):
    tm, d = x_ref.shape
    @pl.when(pl.program_id(0) % tiles_per_seq == 0)
    def _():
        buf_ref[0:SUBLANES, :] = jnp.zeros((SUBLANES, d), F32)

    b_gate, xc = _conv_project(x_ref, w_in_ref, buf_ref)
    y = (cw_ref[0:1, :] * buf_ref[SUBLANES - 2:SUBLANES - 2 + tm, :]
         + cw_ref[1:2, :] * buf_ref[SUBLANES - 1:SUBLANES - 1 + tm, :]
         + cw_ref[2:3, :] * xc)
    _conv_finish(x_ref, b_gate, y, w_out_ref, mg_ref, mb_ref, out_ref)
    tail = buf_ref[tm:tm + SUBLANES, :]
    tail_ref[0] = tail
    buf_ref[0:SUBLANES, :] = tail


def _conv_mixer_prompt(x, w_in, cw, w_out, mg, mb, *, tm, seq):
    t, d = x.shape
    tiles_per_seq = seq // tm
    return pl.pallas_call(
        functools.partial(_conv_prompt_body, tiles_per_seq=tiles_per_seq),
        grid=(t // tm,),
        in_specs=[_rows(tm, d), _resident(w_in.shape), _resident(cw.shape), _resident(w_out.shape),
                  _resident(mg.shape), _resident(mb.shape)],
        out_specs=[_rows(tm, d), pl.BlockSpec((1, SUBLANES, d), lambda i: (i // tiles_per_seq, 0, 0))],
        out_shape=[jax.ShapeDtypeStruct((t, d), F32),
                   jax.ShapeDtypeStruct((t // seq, SUBLANES, d), F32)],
        scratch_shapes=[pltpu.VMEM((tm + SUBLANES, d), F32)],
        compiler_params=_params("arbitrary"),
        name="conv_mixer_prompt",
    )(x, w_in, cw, w_out, mg, mb)


def _conv_sample_body(x_ref, s1_ref, s2_ref, w_in_ref, cw_ref, w_out_ref, mg_ref, mb_ref,
                      out_ref, xc_ref, buf_ref, *, seq):
    tm, d = x_ref.shape
    buf_ref[0:SUBLANES, :] = jnp.zeros((SUBLANES, d), F32)
    b_gate, xc = _conv_project(x_ref, w_in_ref, buf_ref)
    xc_ref[...] = xc
    step = lax.rem(lax.broadcasted_iota(jnp.int32, (tm, 1), 0), seq)
    prev1 = jnp.where(step >= 1, buf_ref[SUBLANES - 1:SUBLANES - 1 + tm, :], s1_ref[...])
    prev2 = jnp.where(step >= 2, buf_ref[SUBLANES - 2:SUBLANES - 2 + tm, :], s2_ref[...])
    y = cw_ref[0:1, :] * prev2 + cw_ref[1:2, :] * prev1 + cw_ref[2:3, :] * xc
    _conv_finish(x_ref, b_gate, y, w_out_ref, mg_ref, mb_ref, out_ref)


def _conv_mixer_sample(x, s1, s2, w_in, cw, w_out, mg, mb, *, seq):
    t, d = x.shape
    args = (x, s1, s2, w_in, cw, w_out, mg, mb)
    return pl.pallas_call(
        functools.partial(_conv_sample_body, seq=seq),
        grid=(1,),
        in_specs=[_resident(a.shape) for a in args],
        out_specs=[_resident((t, d)), _resident((t, d))],
        out_shape=[jax.ShapeDtypeStruct((t, d), F32), jax.ShapeDtypeStruct((t, d), F32)],
        scratch_shapes=[pltpu.VMEM((t + SUBLANES, d), F32)],
        compiler_params=_params("arbitrary"),
        name="conv_mixer_sample",
    )(*args)


def _qkv_body(x_ref, w_ref, k_ref, v_ref, qb_ref, kb_ref, vb_ref, *, scale):
    d = x_ref.shape[1]
    xb = x_ref[...].astype(BF16)
    qb_ref[...] = (_dot(xb, w_ref[:, :d]) * scale).astype(BF16)
    k = _dot(xb, w_ref[:, d:2 * d])
    k_ref[...] = k
    kb_ref[...] = k.astype(BF16)
    v = _dot(xb, w_ref[:, 2 * d:])
    v_ref[...] = v
    vb_ref[...] = v.astype(BF16)


def _qkv_project(x, w_qkv, *, tm):
    t, d = x.shape
    scale = (d // N_HEADS) ** -0.5
    return pl.pallas_call(
        functools.partial(_qkv_body, scale=scale),
        grid=(t // tm,),
        in_specs=[_rows(tm, d), _resident(w_qkv.shape)],
        out_specs=[_rows(tm, d)] * 5,
        out_shape=[jax.ShapeDtypeStruct((t, d), F32)] * 2 + [jax.ShapeDtypeStruct((t, d), BF16)] * 3,
        compiler_params=_params("arbitrary"),
        name="qkv_project",
    )(x, w_qkv)


def _strict_upper_ones(tk):
    row = lax.broadcasted_iota(jnp.int32, (tk, tk), 0)
    col = lax.broadcasted_iota(jnp.int32, (tk, tk), 1)
    return jnp.where(row > col, 1.0, 0.0).astype(BF16)


def _stick_block(z, newer, u_mat, mask):
    sp = jnp.maximum(z, 0.0) + jnp.log(1.0 + jnp.exp(-jnp.abs(z)))
    log_beta = z - sp
    if mask is not None:
        sp = jnp.where(mask, sp, 0.0)
    hi = sp.astype(BF16)
    lo = (sp - hi.astype(F32)).astype(BF16)
    suffix = _dot(hi, u_mat) + _dot(lo, u_mat)
    att = jnp.exp(log_beta - suffix - newer)
    if mask is not None:
        att = jnp.where(mask, att, 0.0)
    newer = newer + suffix[:, 0:1] + sp[:, 0:1]
    return att, newer


def _attn_prompt_body(bias_ref, q_ref, k_ref, v_ref, o_ref, *, tq, tk):
    pair = pl.program_id(1)
    qi = pl.program_id(2)
    half = LANES // 2
    low = lax.broadcasted_iota(jnp.int32, (1, LANES), 1) < half
    q = q_ref[0]
    zero = jnp.zeros_like(q)
    qs = jnp.concatenate([jnp.where(low, q, zero), jnp.where(low, zero, q)], axis=0)
    row = lax.broadcasted_iota(jnp.int32, (2 * tq, 1), 0)
    first = row < tq
    bias = jnp.where(first, bias_ref[2 * pair], bias_ref[2 * pair + 1])
    q_pos = qi * tq + jnp.where(first, row, row - tq)
    u_mat = _strict_upper_ones(tk)

    def block(j, carry, masked):
        newer, acc = carry
        k0 = pl.multiple_of(j * tk, tk)
        kb = k_ref[0, pl.ds(k0, tk), :]
        vb = v_ref[0, pl.ds(k0, tk), :]
        z = lax.dot_general(qs, kb, (((1,), (1,)), ((), ())), preferred_element_type=F32) + bias
        mask = None
        if masked:
            k_pos = k0 + lax.broadcasted_iota(jnp.int32, (1, tk), 1)
            mask = k_pos < q_pos
        att, newer = _stick_block(z, newer, u_mat, mask)
        ab = att.astype(BF16)
        vzero = jnp.zeros_like(vb)
        acc = acc + _dot(ab[:tq], jnp.where(low, vb, vzero)) + _dot(ab[tq:], jnp.where(low, vzero, vb))
        return newer, acc

    j_diag = (qi * tq + tq - 2) // tk
    carry = (jnp.zeros((2 * tq, 1), F32), jnp.zeros((tq, LANES), F32))
    carry = block(j_diag, carry, True)
    carry = lax.fori_loop(0, j_diag, lambda i, c: block(j_diag - 1 - i, c, False), carry)
    o_ref[0] = carry[1].astype(o_ref.dtype)


def _attn_prompt(bias, qb, kb, vb, *, tq, tk):
    nb, seq, d = qb.shape
    assert tq % 2 == 0 and tk % tq == 0 and seq % tk == 0
    q_spec = pl.BlockSpec((1, tq, LANES), lambda b, p, i: (b, i, p))
    kv_spec = pl.BlockSpec((1, seq, LANES), lambda b, p, i: (b, 0, p))
    return pl.pallas_call(
        functools.partial(_attn_prompt_body, tq=tq, tk=tk),
        grid=(nb, d // LANES, seq // tq),
        in_specs=[pl.BlockSpec(memory_space=pltpu.SMEM), q_spec, kv_spec, kv_spec],
        out_specs=q_spec,
        out_shape=jax.ShapeDtypeStruct((nb, seq, d), BF16),
        compiler_params=_params("arbitrary", "arbitrary", "arbitrary"),
        name="attn_prompt",
    )(bias, qb, kb, vb)


def _attn_sample_body(pt_ref, qbd_ref, bias_ref, hm_ref, sel_ref, kn_ref, vn_ref, kp_ref, vp_ref,
                      o_ref, newer_ref, acc_ref, *, dec_seq):
    j = pl.program_id(1)
    qbd = qbd_ref[0]
    m = qbd.shape[0]
    bias = bias_ref[...]

    def block(k, v, mask):
        tk = k.shape[0]
        z = lax.dot_general(qbd, k.astype(BF16), (((1,), (1,)), ((), ())),
                            preferred_element_type=F32) + bias
        att, newer = _stick_block(z, newer_ref[...], _strict_upper_ones(tk), mask)
        newer_ref[...] = newer
        acc_ref[...] += _dot(att.astype(BF16), v.astype(BF16))

    @pl.when(j == 0)
    def _():
        newer_ref[...] = jnp.zeros_like(newer_ref)
        acc_ref[...] = jnp.zeros_like(acc_ref)
        step = lax.rem(lax.broadcasted_iota(jnp.int32, (m, 1), 0), dec_seq)
        mask = lax.broadcasted_iota(jnp.int32, (1, NEW_PAD), 1) < step
        block(kn_ref[0], vn_ref[0], mask)

    block(kp_ref[0], vp_ref[0], None)

    @pl.when(j == pl.num_programs(1) - 1)
    def _():
        own = (acc_ref[...] * hm_ref[...]).astype(BF16)
        o_ref[0] = _dot(sel_ref[...], own).astype(o_ref.dtype)


def _attn_sample(page_table, qbd, bias_rows, hm_rows, sel, k_new, v_new, cache_k, cache_v, *, dec_seq):
    nseq, n_pages = page_table.shape
    _, m, d = qbd.shape
    page = cache_k.shape[1]
    pt_flat = page_table.reshape(-1)

    def page_map(b, j, pt):
        return (pt[b * n_pages + (n_pages - 1 - j)], 0, 0)

    per_seq = lambda b, j, pt: (b, 0, 0)
    const2 = lambda b, j, pt: (0, 0)
    grid_spec = pltpu.PrefetchScalarGridSpec(
        num_scalar_prefetch=1,
        grid=(nseq, n_pages),
        in_specs=[pl.BlockSpec((1, m, d), per_seq),
                  pl.BlockSpec(bias_rows.shape, const2),
                  pl.BlockSpec(hm_rows.shape, const2),
                  pl.BlockSpec(sel.shape, const2),
                  pl.BlockSpec((1, NEW_PAD, d), per_seq),
                  pl.BlockSpec((1, NEW_PAD, d), per_seq),
                  pl.BlockSpec((1, page, d), page_map),
                  pl.BlockSpec((1, page, d), page_map)],
        out_specs=pl.BlockSpec((1, NEW_PAD, d), per_seq),
        scratch_shapes=[pltpu.VMEM((m, 1), F32), pltpu.VMEM((m, d), F32)],
    )
    return pl.pallas_call(
        functools.partial(_attn_sample_body, dec_seq=dec_seq),
        grid_spec=grid_spec,
        out_shape=jax.ShapeDtypeStruct((nseq, NEW_PAD, d), BF16),
        compiler_params=_params("arbitrary", "arbitrary"),
        name="attn_sample",
    )(pt_flat, qbd, bias_rows, hm_rows, sel, k_new, v_new, cache_k, cache_v)


def _proj_ln_body(x_ref, o_ref, w_ref, g_ref, b_ref, out_ref):
    mix = _dot(o_ref[...], w_ref[...])
    out_ref[...] = _layer_norm(ALPHA * x_ref[...] + mix, g_ref[...], b_ref[...])


def _proj_ln(x, o, w_out, mg, mb, *, tm):
    t, d = x.shape
    return pl.pallas_call(
        _proj_ln_body,
        grid=(t // tm,),
        in_specs=[_rows(tm, d), _rows(tm, d), _resident(w_out.shape), _resident(mg.shape),
                  _resident(mb.shape)],
        out_specs=_rows(tm, d),
        out_shape=jax.ShapeDtypeStruct((t, d), F32),
        compiler_params=_params("arbitrary"),
        name="attn_out_proj",
    )(x, o, w_out, mg, mb)


def _ff_chunks(d_ff, width):
    return tuple((c0, min(c0 + width, d_ff)) for c0 in range(0, d_ff, width))


def _ffn_body(x_ref, p_ref, w_up_ref, w_down_ref, g_ref, b_ref, wg_ref, wp_ref, out_ref, acc_ref):
    d_ff = w_down_ref.shape[0]
    x = x_ref[...]
    xb = x.astype(BF16)
    for idx, (c0, c1) in enumerate(_ff_chunks(d_ff, x.shape[1])):
        gate = _dot(xb, w_up_ref[:, c0:c1])
        up = _dot(xb, w_up_ref[:, d_ff + c0:d_ff + c1])
        hidden = (gate * _sigmoid(gate) * up).astype(BF16)
        part = _dot(hidden, w_down_ref[c0:c1, :])
        if idx == 0:
            acc_ref[...] = part
        else:
            acc_ref[...] += part
    x2 = _layer_norm(ALPHA * x + acc_ref[...], g_ref[...], b_ref[...])
    ple_gate = _sigmoid(_dot(x2.astype(BF16), wg_ref[...]))
    out_ref[...] = x2 + ple_gate * _dot(p_ref[...].astype(BF16), wp_ref[...])


def _ffn_ple(x, p, w_up, w_down, g, b, w_gate, w_proj, *, tm):
    t, d = x.shape
    return pl.pallas_call(
        _ffn_body,
        grid=(t // tm,),
        in_specs=[_rows(tm, d), _rows(tm, p.shape[1]), _resident(w_up.shape), _resident(w_down.shape),
                  _resident(g.shape), _resident(b.shape), _resident(w_gate.shape),
                  _resident(w_proj.shape)],
        out_specs=_rows(tm, d),
        out_shape=jax.ShapeDtypeStruct((t, d), F32),
        scratch_shapes=[pltpu.VMEM((tm, d), F32)],
        compiler_params=_params("arbitrary"),
        name="ffn_ple",
    )(x, p, w_up, w_down, g, b, w_gate, w_proj)


def _row(vec):
    return vec.reshape(1, -1).astype(F32)


def _head_mask(d):
    head_of_lane = jnp.arange(d, dtype=jnp.int32) // (d // N_HEADS)
    return (head_of_lane[None, :] == jnp.arange(N_HEADS, dtype=jnp.int32)[:, None])


def _trunk(x, p, W, *, nseq, seq, tm, sample):
    t, d = x.shape
    chunk_v, conv_new, k_new, v_new = [], [], [], []
    for i in range(DEPTH):
        j, kind = divmod(i, N_MIXERS)
        mg, mb = _row(W["ln_mix_g"][i]), _row(W["ln_mix_b"][i])
        if kind == 0:
            w_s = jnp.where(jnp.tril(jnp.ones((CHUNK, CHUNK), bool))[None], W["a_w_s"][j], 0.0)
            b_s = W["a_b_s"][j]
            if sample is None:
                w_mix, b_rows = w_s, b_s.T
            else:
                eye = jnp.eye(t // seq, dtype=F32)
                w_mix = jax.vmap(lambda w: jnp.kron(eye, w[:seq, :seq]))(w_s)
                b_rows = jnp.tile(b_s[:, :seq].T, (t // seq, 1))
            bmap = jnp.repeat(b_rows.astype(F32), d // A_GROUPS, axis=1)
            x, v_rows = _chunk_mixer(
                x, W["a_w_in"][j].astype(BF16), _row(W["a_ln_g"][j]), _row(W["a_ln_b"][j]),
                w_mix.astype(BF16), bmap, W["a_w_out"][j].astype(BF16), mg, mb,
                tm=tm, emit_v=sample is not None)
            chunk_v.append(v_rows)
        elif kind == 1:
            w_in, cw, w_out = W["b_w_in"][j].astype(BF16), W["b_conv"][j], W["b_w_out"][j].astype(BF16)
            if sample is None:
                x, tail = _conv_mixer_prompt(x, w_in, cw, w_out, mg, mb, tm=tm, seq=seq)
                conv_new.append(tail[:, SUBLANES - (CONV_W - 1):])
            else:
                st = sample["state_conv"][j]
                zeros = jnp.zeros((nseq, seq, d), F32)
                s1 = zeros.at[:, 0].set(st[:, 1]).reshape(t, d)
                s2 = zeros.at[:, 0].set(st[:, 0]).at[:, 1].set(st[:, 1]).reshape(t, d)
                x, xc = _conv_mixer_sample(x, s1, s2, w_in, cw, w_out, mg, mb, seq=seq)
                conv_new.append(xc.reshape(nseq, seq, d)[:, seq - (CONV_W - 1):])
        else:
            k, v, qb, kb, vb = _qkv_project(x, W["c_w_qkv"][j].astype(BF16), tm=tm)
            bias = W["c_b_beta"][j].astype(F32)
            if sample is None:
                o = _attn_prompt(bias, qb.reshape(nseq, seq, d), kb.reshape(nseq, seq, d),
                                 vb.reshape(nseq, seq, d), tq=ATTN_TQ, tk=ATTN_TK).reshape(t, d)
            else:
                hm = _head_mask(d)
                qbd = jnp.where(hm[None, :, None, :], qb.reshape(nseq, 1, seq, d), 0)
                qbd = qbd.reshape(nseq, N_HEADS * seq, d)
                bias_rows = jnp.repeat(bias, seq)[:, None]
                hm_rows = jnp.repeat(hm.astype(F32), seq, axis=0)
                sel = (jnp.arange(NEW_PAD)[:, None] == (jnp.arange(N_HEADS * seq) % seq)[None, :])
                pad = ((0, 0), (0, NEW_PAD - seq), (0, 0))
                n_pool, page = sample["cache_k"].shape[1:3]
                o = _attn_sample(
                    sample["page_table"], qbd, bias_rows, hm_rows, sel.astype(BF16),
                    jnp.pad(k.reshape(nseq, seq, d), pad), jnp.pad(v.reshape(nseq, seq, d), pad),
                    sample["cache_k"][j].reshape(n_pool, page, d),
                    sample["cache_v"][j].reshape(n_pool, page, d), dec_seq=seq)
                o = o[:, :seq].reshape(t, d)
            x = _proj_ln(x, o, W["c_w_out"][j].astype(BF16), mg, mb, tm=tm)
            k_new.append(k)
            v_new.append(v)
        x = _ffn_ple(x, p[i], W["f_w_up"][i].astype(BF16), W["f_w_down"][i].astype(BF16),
                     _row(W["ln_ffn_g"][i]), _row(W["ln_ffn_b"][i]),
                     W["ple_w_gate"][i].astype(BF16), W["ple_w_proj"][i].astype(BF16), tm=tm)
    return x, chunk_v, conv_new, k_new, v_new


def kernel(x_prompt, x_sample, cache_k, cache_v, state_conv, page_table, p_prompt, p_sample, a_w_in, a_ln_g, a_ln_b, a_w_s, a_b_s, a_w_out, b_w_in, b_conv, b_w_out, c_w_qkv, c_b_beta, c_w_out, ln_mix_g, ln_mix_b, ln_ffn_g, ln_ffn_b, f_w_up, f_w_down, ple_w_gate, ple_w_proj):
    W = dict(a_w_in=a_w_in, a_ln_g=a_ln_g, a_ln_b=a_ln_b, a_w_s=a_w_s, a_b_s=a_b_s, a_w_out=a_w_out,
             b_w_in=b_w_in, b_conv=b_conv, b_w_out=b_w_out, c_w_qkv=c_w_qkv, c_b_beta=c_b_beta,
             c_w_out=c_w_out, ln_mix_g=ln_mix_g, ln_mix_b=ln_mix_b, ln_ffn_g=ln_ffn_g, ln_ffn_b=ln_ffn_b,
             f_w_up=f_w_up, f_w_down=f_w_down, ple_w_gate=ple_w_gate, ple_w_proj=ple_w_proj)
    nb, seq, d = x_prompt.shape
    nd, dseq, _ = x_sample.shape
    heads_shape = (N_HEADS, d // N_HEADS)

    yp, _, conv_p, k_p, v_p = _trunk(
        x_prompt.reshape(nb * seq, d), p_prompt.reshape(DEPTH, nb * seq, -1), W,
        nseq=nb, seq=seq, tm=min(ROW_TILE, seq), sample=None)
    state = dict(state_conv=state_conv, cache_k=cache_k, cache_v=cache_v, page_table=page_table)
    ys, chunk_v, conv_s, k_s, v_s = _trunk(
        x_sample.reshape(nd * dseq, d), p_sample.reshape(DEPTH, nd * dseq, -1), W,
        nseq=nd, seq=dseq, tm=nd * dseq, sample=state)

    def kv(parts, n, s):
        return jnp.stack([a.reshape((n, s) + heads_shape) for a in parts])

    return (yp.reshape(nb, seq, d), ys.reshape(nd, dseq, d),
            kv(k_p, nb, seq), kv(v_p, nb, seq), kv(k_s, nd, dseq), kv(v_s, nd, dseq),
            jnp.stack(conv_p), jnp.stack(conv_s),
            jnp.stack([a.reshape(nd, dseq, d) for a in chunk_v]))
```

```python
import functools

import jax
import jax.numpy as jnp
from jax import lax
from jax.experimental import pallas as pl
from jax.experimental.pallas import tpu as pltpu

F32 = jnp.float32
BF16 = jnp.bfloat16

DEPTH = 4
N_MIXERS = 3
CHUNK = 128
A_GROUPS = 4
CONV_W = 3
N_HEADS = 16
ALPHA = (2 * DEPTH) ** 0.25
LN_EPS = 1e-5

LANES = 128
SUBLANES = 8
ROW_TILE = 512
ATTN_TQ = 512
ATTN_TK = 256
ATTN_UNROLL = 2
LOG2E = 1.4426950408889634
SAMPLE_PAGES = 4
NEW_PAD = 16
VMEM_LIMIT = 56 * 1024 * 1024


def _params(*semantics):
    return pltpu.CompilerParams(dimension_semantics=semantics, vmem_limit_bytes=VMEM_LIMIT)


def _resident(shape):
    zeros = (0,) * len(shape)
    return pl.BlockSpec(shape, lambda *_: zeros, pipeline_mode=pl.Buffered(1))


def _rows(tm, width):
    return pl.BlockSpec((tm, width), lambda i: (i, 0))


def _dot(a, b):
    return jnp.dot(a, b, preferred_element_type=F32)


def _layer_norm(x, g, b):
    mu = jnp.mean(x, axis=-1, keepdims=True)
    xc = x - mu
    var = jnp.mean(xc * xc, axis=-1, keepdims=True)
    return xc * lax.rsqrt(var + LN_EPS) * g + b


def _sigmoid(x):
    return 1.0 / (1.0 + jnp.exp(-x))


def _chunk_mixer_body(x_ref, w_in_ref, lng_ref, lnb_ref, ws_ref, bmap_ref, w_out_ref, mg_ref, mb_ref,
                      out_ref, *rest, emit_v):
    if emit_v:
        v_out_ref, u_ref, v_ref, s_ref = rest
    else:
        u_ref, v_ref, s_ref = rest
    tm, d = x_ref.shape
    gd = d // A_GROUPS
    x = x_ref[...]
    xb = x.astype(BF16)
    u_ref[...] = jax.nn.gelu(_dot(xb, w_in_ref[:, :d]))
    v = _layer_norm(jax.nn.gelu(_dot(xb, w_in_ref[:, d:])), lng_ref[...], lnb_ref[...])
    if emit_v:
        v_out_ref[...] = v
    v_ref[...] = v.astype(BF16)
    for c in range(tm // CHUNK):
        r0 = c * CHUNK
        for g in range(A_GROUPS):
            c0 = g * gd
            s_ref[r0:r0 + CHUNK, c0:c0 + gd] = (
                _dot(ws_ref[g], v_ref[r0:r0 + CHUNK, c0:c0 + gd]) + bmap_ref[:, c0:c0 + gd])
    gated = (u_ref[...] * s_ref[...]).astype(BF16)
    mix = _dot(gated, w_out_ref[...])
    out_ref[...] = _layer_norm(ALPHA * x + mix, mg_ref[...], mb_ref[...])


def _chunk_mixer(x, w_in, ln_g, ln_b, w_s, bmap, w_out, mg, mb, *, tm, emit_v):
    t, d = x.shape
    out_shape = [jax.ShapeDtypeStruct((t, d), F32)]
    out_specs = [_rows(tm, d)]
    if emit_v:
        out_shape.append(jax.ShapeDtypeStruct((t, d), F32))
        out_specs.append(_rows(tm, d))
    res = pl.pallas_call(
        functools.partial(_chunk_mixer_body, emit_v=emit_v),
        grid=(t // tm,),
        in_specs=[_rows(tm, d), _resident(w_in.shape), _resident(ln_g.shape), _resident(ln_b.shape),
                  _resident(w_s.shape), _resident(bmap.shape), _resident(w_out.shape),
                  _resident(mg.shape), _resident(mb.shape)],
        out_specs=out_specs,
        out_shape=out_shape,
        scratch_shapes=[pltpu.VMEM((tm, d), F32), pltpu.VMEM((tm, d), BF16), pltpu.VMEM((tm, d), F32)],
        compiler_params=_params("arbitrary"),
        name="chunk_mixer",
    )(x, w_in, ln_g, ln_b, w_s, bmap, w_out, mg, mb)
    return res if emit_v else (res[0], None)


def _conv_project(x_ref, w_in_ref, buf_ref):
    tm, d = x_ref.shape
    xb = x_ref[...].astype(BF16)
    b_gate = _dot(xb, w_in_ref[:, :d])
    xc = _dot(xb, w_in_ref[:, d:2 * d]) * _dot(xb, w_in_ref[:, 2 * d:])
    buf_ref[SUBLANES:SUBLANES + tm, :] = xc
    return b_gate, xc


def _conv_finish(x_ref, b_gate, y, w_out_ref, mg_ref, mb_ref, out_ref):
    mix = _dot((b_gate * y).astype(BF16), w_out_ref[...])
    out_ref[...] = _layer_norm(ALPHA * x_ref[...] + mix, mg_ref[...], mb_ref[...])


def _conv_prompt_body(x_ref, w_in_ref, cw_ref, w_out_ref, mg_ref, mb_ref, out_ref, tail_ref, buf_ref,
                      *, tiles_per_seq):
    tm, d = x_ref.shape
    @pl.when(pl.program_id(0) % tiles_per_seq == 0)
    def _():
        buf_ref[0:SUBLANES, :] = jnp.zeros((SUBLANES, d), F32)

    b_gate, xc = _conv_project(x_ref, w_in_ref, buf_ref)
    y = (cw_ref[0:1, :] * buf_ref[SUBLANES - 2:SUBLANES - 2 + tm, :]
         + cw_ref[1:2, :] * buf_ref[SUBLANES - 1:SUBLANES - 1 + tm, :]
         + cw_ref[2:3, :] * xc)
    _conv_finish(x_ref, b_gate, y, w_out_ref, mg_ref, mb_ref, out_ref)
    tail = buf_ref[tm:tm + SUBLANES, :]
    tail_ref[0] = tail
    buf_ref[0:SUBLANES, :] = tail


def _conv_mixer_prompt(x, w_in, cw, w_out, mg, mb, *, tm, seq):
    t, d = x.shape
    tiles_per_seq = seq // tm
    return pl.pallas_call(
        functools.partial(_conv_prompt_body, tiles_per_seq=tiles_per_seq),
        grid=(t // tm,),
        in_specs=[_rows(tm, d), _resident(w_in.shape), _resident(cw.shape), _resident(w_out.shape),
                  _resident(mg.shape), _resident(mb.shape)],
        out_specs=[_rows(tm, d), pl.BlockSpec((1, SUBLANES, d), lambda i: (i // tiles_per_seq, 0, 0))],
        out_shape=[jax.ShapeDtypeStruct((t, d), F32),
                   jax.ShapeDtypeStruct((t // seq, SUBLANES, d), F32)],
        scratch_shapes=[pltpu.VMEM((tm + SUBLANES, d), F32)],
        compiler_params=_params("arbitrary"),
        name="conv_mixer_prompt",
    )(x, w_in, cw, w_out, mg, mb)


def _conv_sample_body(x_ref, s1_ref, s2_ref, w_in_ref, cw_ref, w_out_ref, mg_ref, mb_ref,
                      out_ref, xc_ref, buf_ref, *, seq):
    tm, d = x_ref.shape
    buf_ref[0:SUBLANES, :] = jnp.zeros((SUBLANES, d), F32)
    b_gate, xc = _conv_project(x_ref, w_in_ref, buf_ref)
    xc_ref[...] = xc
    step = lax.rem(lax.broadcasted_iota(jnp.int32, (tm, 1), 0), seq)
    prev1 = jnp.where(step >= 1, buf_ref[SUBLANES - 1:SUBLANES - 1 + tm, :], s1_ref[...])
    prev2 = jnp.where(step >= 2, buf_ref[SUBLANES - 2:SUBLANES - 2 + tm, :], s2_ref[...])
    y = cw_ref[0:1, :] * prev2 + cw_ref[1:2, :] * prev1 + cw_ref[2:3, :] * xc
    _conv_finish(x_ref, b_gate, y, w_out_ref, mg_ref, mb_ref, out_ref)


def _conv_mixer_sample(x, s1, s2, w_in, cw, w_out, mg, mb, *, seq):
    t, d = x.shape
    args = (x, s1, s2, w_in, cw, w_out, mg, mb)
    return pl.pallas_call(
        functools.partial(_conv_sample_body, seq=seq),
        grid=(1,),
        in_specs=[_resident(a.shape) for a in args],
        out_specs=[_resident((t, d)), _resident((t, d))],
        out_shape=[jax.ShapeDtypeStruct((t, d), F32), jax.ShapeDtypeStruct((t, d), F32)],
        scratch_shapes=[pltpu.VMEM((t + SUBLANES, d), F32)],
        compiler_params=_params("arbitrary"),
        name="conv_mixer_sample",
    )(*args)


def _qkv_body(x_ref, w_ref, k_ref, v_ref, qb_ref, kb_ref, vb_ref, *, scale):
    d = x_ref.shape[1]
    xb = x_ref[...].astype(BF16)
    qb_ref[...] = (_dot(xb, w_ref[:, :d]) * scale).astype(BF16)
    k = _dot(xb, w_ref[:, d:2 * d])
    k_ref[...] = k
    kb_ref[...] = k.astype(BF16)
    v = _dot(xb, w_ref[:, 2 * d:])
    v_ref[...] = v
    vb_ref[...] = v.astype(BF16)


def _qkv_project(x, w_qkv, *, tm):
    t, d = x.shape
    scale = (d // N_HEADS) ** -0.5
    return pl.pallas_call(
        functools.partial(_qkv_body, scale=scale),
        grid=(t // tm,),
        in_specs=[_rows(tm, d), _resident(w_qkv.shape)],
        out_specs=[_rows(tm, d)] * 5,
        out_shape=[jax.ShapeDtypeStruct((t, d), F32)] * 2 + [jax.ShapeDtypeStruct((t, d), BF16)] * 3,
        compiler_params=_params("arbitrary"),
        name="qkv_project",
    )(x, w_qkv)


_CONTRACT_LAST = (((1,), (1,)), ((), ()))


def _suffix_ones(tk):
    row = lax.broadcasted_iota(jnp.int32, (tk, tk), 0)
    col = lax.broadcasted_iota(jnp.int32, (tk, tk), 1)
    return jnp.where(row > col, 1.0, 0.0).astype(BF16)


def _stick_scores(z, u, mask):
    sp = jnp.maximum(z, 0.0) + jnp.log(1.0 + jnp.exp2(jnp.abs(z) * -LOG2E))
    log_beta = z - sp
    if mask is not None:
        sp = jnp.where(mask, sp, 0.0)
    suffix = _dot(sp.astype(BF16), u)
    return log_beta, suffix, suffix[:, 0:1] + sp[:, 0:1]


def _stick_weights(log_beta, suffix, newer, mask):
    att = jnp.exp(log_beta - suffix - newer)
    if mask is not None:
        att = jnp.where(mask, att, 0.0)
    return att.astype(BF16)


def _attn_prompt_body(bias_ref, q_ref, k_ref, v_ref, o_ref, newer_ref, acc_ref, *, tq, tk, unroll):
    pair = pl.program_id(1)
    qi = pl.program_id(2)
    half = LANES // 2
    low = lax.broadcasted_iota(jnp.int32, (1, LANES), 1) < half
    q = q_ref[0]
    zero = jnp.zeros_like(q)
    qs = jnp.concatenate([jnp.where(low, q, zero), jnp.where(low, zero, q)], axis=0)
    row = lax.broadcasted_iota(jnp.int32, (2 * tq, 1), 0)
    first = row < tq
    bias = jnp.where(first, bias_ref[2 * pair], bias_ref[2 * pair + 1])
    q_pos = qi * tq + jnp.where(first, row, row - tq)
    u_mat = _suffix_ones(tk)

    def scores(j, masked):
        k0 = pl.multiple_of(j * tk, tk)
        z = lax.dot_general(qs, k_ref[0, pl.ds(k0, tk), :], _CONTRACT_LAST,
                            preferred_element_type=F32) + bias
        mask = None
        if masked:
            mask = (k0 + lax.broadcasted_iota(jnp.int32, (1, tk), 1)) < q_pos
        return _stick_scores(z, u_mat, mask) + (mask,)

    def accumulate(j, log_beta, suffix, newer, mask):
        k0 = pl.multiple_of(j * tk, tk)
        ab = _stick_weights(log_beta, suffix, newer, mask)
        vb = v_ref[0, pl.ds(k0, tk), :]
        vzero = jnp.zeros_like(vb)
        a2 = jnp.concatenate([ab[:tq], ab[tq:]], axis=1)
        v2 = jnp.concatenate([jnp.where(low, vb, vzero), jnp.where(low, vzero, vb)], axis=0)
        acc_ref[...] += _dot(a2, v2)

    def run_blocks(j_newest, n, masked):
        halves = [scores(j_newest - i, masked) for i in range(n)]
        newer = newer_ref[...]
        for i, (log_beta, suffix, total, mask) in enumerate(halves):
            accumulate(j_newest - i, log_beta, suffix, newer, mask)
            newer = newer + total
        newer_ref[...] = newer

    newer_ref[...] = jnp.zeros_like(newer_ref)
    acc_ref[...] = jnp.zeros_like(acc_ref)
    per_tile = tq // tk
    run_blocks(qi * per_tile + per_tile - 1, per_tile, True)

    def group(i, _):
        run_blocks(qi * per_tile - 1 - unroll * i, unroll, False)
        return 0

    lax.fori_loop(0, qi * (per_tile // unroll), group, 0)
    o_ref[0] = acc_ref[...].astype(o_ref.dtype)


def _attn_prompt(bias, qb, kb, vb, *, tq, tk, unroll):
    nb, seq, d = qb.shape
    assert seq % tq == 0 and tq % tk == 0 and (tq // tk) % unroll == 0
    q_spec = pl.BlockSpec((1, tq, LANES), lambda b, p, i: (b, i, p))
    kv_spec = pl.BlockSpec((1, seq, LANES), lambda b, p, i: (b, 0, p))
    return pl.pallas_call(
        functools.partial(_attn_prompt_body, tq=tq, tk=tk, unroll=unroll),
        grid=(nb, d // LANES, seq // tq),
        in_specs=[pl.BlockSpec(memory_space=pltpu.SMEM), q_spec, kv_spec, kv_spec],
        out_specs=q_spec,
        out_shape=jax.ShapeDtypeStruct((nb, seq, d), BF16),
        scratch_shapes=[pltpu.VMEM((2 * tq, 1), F32), pltpu.VMEM((tq, LANES), F32)],
        compiler_params=_params("arbitrary", "arbitrary", "arbitrary"),
        name="attn_prompt",
    )(bias, qb, kb, vb)


def _attn_sample_body(pt_ref, qbd_ref, bias_ref, hm_ref, sel_ref, kn_ref, vn_ref, *rest, dec_seq, pages):
    kt_refs, vt_refs = rest[:pages], rest[pages:2 * pages]
    o_ref, newer_ref, acc_ref = rest[2 * pages:]
    j = pl.program_id(1)
    qbd = qbd_ref[0]
    m = qbd.shape[0]
    bias = bias_ref[...]

    @pl.when(j == 0)
    def _():
        step = lax.rem(lax.broadcasted_iota(jnp.int32, (m, 1), 0), dec_seq)
        mask = lax.broadcasted_iota(jnp.int32, (1, NEW_PAD), 1) < step
        z = lax.dot_general(qbd, kn_ref[0].astype(BF16), _CONTRACT_LAST,
                            preferred_element_type=F32) + bias
        log_beta, suffix, total = _stick_scores(z, _suffix_ones(NEW_PAD), mask)
        ab = _stick_weights(log_beta, suffix, jnp.zeros((m, 1), F32), mask)
        acc_ref[...] = _dot(ab, vn_ref[0].astype(BF16))
        newer_ref[...] = total

    def two_pages(refs, b):
        return jnp.concatenate([refs[2 * b + 1][0].astype(BF16), refs[2 * b][0].astype(BF16)], axis=1)

    u = _suffix_ones(2 * kt_refs[0].shape[2])
    blocks = [_stick_scores(_dot(qbd, two_pages(kt_refs, b)) + bias, u, None)
              for b in range(pages // 2)]
    newer = newer_ref[...]
    for b, (log_beta, suffix, total) in enumerate(blocks):
        ab = _stick_weights(log_beta, suffix, newer, None)
        acc_ref[...] += lax.dot_general(ab, two_pages(vt_refs, b), _CONTRACT_LAST,
                                        preferred_element_type=F32)
        newer = newer + total
    newer_ref[...] = newer

    @pl.when(j == pl.num_programs(1) - 1)
    def _():
        own = (acc_ref[...] * hm_ref[...]).astype(BF16)
        o_ref[0] = _dot(sel_ref[...], own).astype(o_ref.dtype)


def _attn_sample(page_table, qbd, bias_rows, hm_rows, sel, k_new, v_new, cache_kt, cache_vt, *, dec_seq,
                 pages):
    nseq, n_pages = page_table.shape
    _, m, d = qbd.shape
    page = cache_kt.shape[2]
    assert pages % 2 == 0 and n_pages % pages == 0
    pt_flat = page_table.reshape(-1)

    def page_spec(u):
        return pl.BlockSpec(
            (1, d, page), lambda b, j, pt: (pt[b * n_pages + n_pages - 1 - (j * pages + u)], 0, 0))

    per_seq = lambda b, j, pt: (b, 0, 0)
    const2 = lambda b, j, pt: (0, 0)
    grid_spec = pltpu.PrefetchScalarGridSpec(
        num_scalar_prefetch=1,
        grid=(nseq, n_pages // pages),
        in_specs=[pl.BlockSpec((1, m, d), per_seq),
                  pl.BlockSpec(bias_rows.shape, const2),
                  pl.BlockSpec(hm_rows.shape, const2),
                  pl.BlockSpec(sel.shape, const2),
                  pl.BlockSpec((1, NEW_PAD, d), per_seq),
                  pl.BlockSpec((1, NEW_PAD, d), per_seq)]
                 + [page_spec(u) for u in range(pages)] * 2,
        out_specs=pl.BlockSpec((1, NEW_PAD, d), per_seq),
        scratch_shapes=[pltpu.VMEM((m, 1), F32), pltpu.VMEM((m, d), F32)],
    )
    return pl.pallas_call(
        functools.partial(_attn_sample_body, dec_seq=dec_seq, pages=pages),
        grid_spec=grid_spec,
        out_shape=jax.ShapeDtypeStruct((nseq, NEW_PAD, d), BF16),
        compiler_params=_params("arbitrary", "arbitrary"),
        name="attn_sample",
    )(pt_flat, qbd, bias_rows, hm_rows, sel, k_new, v_new, *([cache_kt] * pages), *([cache_vt] * pages))


def _proj_ln_body(x_ref, o_ref, w_ref, g_ref, b_ref, out_ref):
    mix = _dot(o_ref[...], w_ref[...])
    out_ref[...] = _layer_norm(ALPHA * x_ref[...] + mix, g_ref[...], b_ref[...])


def _proj_ln(x, o, w_out, mg, mb, *, tm):
    t, d = x.shape
    return pl.pallas_call(
        _proj_ln_body,
        grid=(t // tm,),
        in_specs=[_rows(tm, d), _rows(tm, d), _resident(w_out.shape), _resident(mg.shape),
                  _resident(mb.shape)],
        out_specs=_rows(tm, d),
        out_shape=jax.ShapeDtypeStruct((t, d), F32),
        compiler_params=_params("arbitrary"),
        name="attn_out_proj",
    )(x, o, w_out, mg, mb)


def _ff_chunks(d_ff, width):
    return tuple((c0, min(c0 + width, d_ff)) for c0 in range(0, d_ff, width))


def _ffn_body(x_ref, p_ref, w_up_ref, w_down_ref, g_ref, b_ref, wg_ref, wp_ref, out_ref, acc_ref):
    d_ff = w_down_ref.shape[0]
    x = x_ref[...]
    xb = x.astype(BF16)
    for idx, (c0, c1) in enumerate(_ff_chunks(d_ff, x.shape[1])):
        gate = _dot(xb, w_up_ref[:, c0:c1])
        up = _dot(xb, w_up_ref[:, d_ff + c0:d_ff + c1])
        hidden = (gate * _sigmoid(gate) * up).astype(BF16)
        part = _dot(hidden, w_down_ref[c0:c1, :])
        if idx == 0:
            acc_ref[...] = part
        else:
            acc_ref[...] += part
    x2 = _layer_norm(ALPHA * x + acc_ref[...], g_ref[...], b_ref[...])
    ple_gate = _sigmoid(_dot(x2.astype(BF16), wg_ref[...]))
    out_ref[...] = x2 + ple_gate * _dot(p_ref[...].astype(BF16), wp_ref[...])


def _ffn_ple(x, p, w_up, w_down, g, b, w_gate, w_proj, *, tm):
    t, d = x.shape
    return pl.pallas_call(
        _ffn_body,
        grid=(t // tm,),
        in_specs=[_rows(tm, d), _rows(tm, p.shape[1]), _resident(w_up.shape), _resident(w_down.shape),
                  _resident(g.shape), _resident(b.shape), _resident(w_gate.shape),
                  _resident(w_proj.shape)],
        out_specs=_rows(tm, d),
        out_shape=jax.ShapeDtypeStruct((t, d), F32),
        scratch_shapes=[pltpu.VMEM((tm, d), F32)],
        compiler_params=_params("arbitrary"),
        name="ffn_ple",
    )(x, p, w_up, w_down, g, b, w_gate, w_proj)


def _row(vec):
    return vec.reshape(1, -1).astype(F32)


def _head_mask(d):
    head_of_lane = jnp.arange(d, dtype=jnp.int32) // (d // N_HEADS)
    return (head_of_lane[None, :] == jnp.arange(N_HEADS, dtype=jnp.int32)[:, None])


def _trunk(x, p, W, *, nseq, seq, tm, sample):
    t, d = x.shape
    chunk_v, conv_new, k_new, v_new = [], [], [], []
    for i in range(DEPTH):
        j, kind = divmod(i, N_MIXERS)
        mg, mb = _row(W["ln_mix_g"][i]), _row(W["ln_mix_b"][i])
        if kind == 0:
            w_s = jnp.where(jnp.tril(jnp.ones((CHUNK, CHUNK), bool))[None], W["a_w_s"][j], 0.0)
            b_s = W["a_b_s"][j]
            if sample is None:
                w_mix, b_rows = w_s, b_s.T
            else:
                eye = jnp.eye(t // seq, dtype=F32)
                w_mix = jax.vmap(lambda w: jnp.kron(eye, w[:seq, :seq]))(w_s)
                b_rows = jnp.tile(b_s[:, :seq].T, (t // seq, 1))
            bmap = jnp.repeat(b_rows.astype(F32), d // A_GROUPS, axis=1)
            x, v_rows = _chunk_mixer(
                x, W["a_w_in"][j].astype(BF16), _row(W["a_ln_g"][j]), _row(W["a_ln_b"][j]),
                w_mix.astype(BF16), bmap, W["a_w_out"][j].astype(BF16), mg, mb,
                tm=tm, emit_v=sample is not None)
            chunk_v.append(v_rows)
        elif kind == 1:
            w_in, cw, w_out = W["b_w_in"][j].astype(BF16), W["b_conv"][j], W["b_w_out"][j].astype(BF16)
            if sample is None:
                x, tail = _conv_mixer_prompt(x, w_in, cw, w_out, mg, mb, tm=tm, seq=seq)
                conv_new.append(tail[:, SUBLANES - (CONV_W - 1):])
            else:
                st = sample["state_conv"][j]
                zeros = jnp.zeros((nseq, seq, d), F32)
                s1 = zeros.at[:, 0].set(st[:, 1]).reshape(t, d)
                s2 = zeros.at[:, 0].set(st[:, 0]).at[:, 1].set(st[:, 1]).reshape(t, d)
                x, xc = _conv_mixer_sample(x, s1, s2, w_in, cw, w_out, mg, mb, seq=seq)
                conv_new.append(xc.reshape(nseq, seq, d)[:, seq - (CONV_W - 1):])
        else:
            k, v, qb, kb, vb = _qkv_project(x, W["c_w_qkv"][j].astype(BF16), tm=tm)
            bias = W["c_b_beta"][j].astype(F32)
            if sample is None:
                o = _attn_prompt(bias, qb.reshape(nseq, seq, d), kb.reshape(nseq, seq, d),
                                 vb.reshape(nseq, seq, d), tq=ATTN_TQ, tk=ATTN_TK,
                                 unroll=ATTN_UNROLL).reshape(t, d)
            else:
                hm = _head_mask(d)
                qbd = jnp.where(hm[None, :, None, :], qb.reshape(nseq, 1, seq, d), 0)
                qbd = qbd.reshape(nseq, N_HEADS * seq, d)
                bias_rows = jnp.repeat(bias, seq)[:, None]
                hm_rows = jnp.repeat(hm.astype(F32), seq, axis=0)
                sel = (jnp.arange(NEW_PAD)[:, None] == (jnp.arange(N_HEADS * seq) % seq)[None, :])
                pad = ((0, 0), (0, NEW_PAD - seq), (0, 0))
                n_pool, page = sample["cache_k"].shape[1:3]
                transposed = lambda c: jnp.transpose(c[j], (0, 2, 3, 1)).reshape(n_pool, d, page)
                o = _attn_sample(
                    sample["page_table"], qbd, bias_rows, hm_rows, sel.astype(BF16),
                    jnp.pad(k.reshape(nseq, seq, d), pad), jnp.pad(v.reshape(nseq, seq, d), pad),
                    transposed(sample["cache_k"]), transposed(sample["cache_v"]), dec_seq=seq,
                    pages=SAMPLE_PAGES)
                o = o[:, :seq].reshape(t, d)
            x = _proj_ln(x, o, W["c_w_out"][j].astype(BF16), mg, mb, tm=tm)
            k_new.append(k)
            v_new.append(v)
        x = _ffn_ple(x, p[i], W["f_w_up"][i].astype(BF16), W["f_w_down"][i].astype(BF16),
                     _row(W["ln_ffn_g"][i]), _row(W["ln_ffn_b"][i]),
                     W["ple_w_gate"][i].astype(BF16), W["ple_w_proj"][i].astype(BF16), tm=tm)
    return x, chunk_v, conv_new, k_new, v_new


def kernel(x_prompt, x_sample, cache_k, cache_v, state_conv, page_table, p_prompt, p_sample, a_w_in, a_ln_g, a_ln_b, a_w_s, a_b_s, a_w_out, b_w_in, b_conv, b_w_out, c_w_qkv, c_b_beta, c_w_out, ln_mix_g, ln_mix_b, ln_ffn_g, ln_ffn_b, f_w_up, f_w_down, ple_w_gate, ple_w_proj):
    W = dict(a_w_in=a_w_in, a_ln_g=a_ln_g, a_ln_b=a_ln_b, a_w_s=a_w_s, a_b_s=a_b_s, a_w_out=a_w_out,
             b_w_in=b_w_in, b_conv=b_conv, b_w_out=b_w_out, c_w_qkv=c_w_qkv, c_b_beta=c_b_beta,
             c_w_out=c_w_out, ln_mix_g=ln_mix_g, ln_mix_b=ln_mix_b, ln_ffn_g=ln_ffn_g, ln_ffn_b=ln_ffn_b,
             f_w_up=f_w_up, f_w_down=f_w_down, ple_w_gate=ple_w_gate, ple_w_proj=ple_w_proj)
    nb, seq, d = x_prompt.shape
    nd, dseq, _ = x_sample.shape
    heads_shape = (N_HEADS, d // N_HEADS)

    yp, _, conv_p, k_p, v_p = _trunk(
        x_prompt.reshape(nb * seq, d), p_prompt.reshape(DEPTH, nb * seq, -1), W,
        nseq=nb, seq=seq, tm=min(ROW_TILE, seq), sample=None)
    state = dict(state_conv=state_conv, cache_k=cache_k, cache_v=cache_v, page_table=page_table)
    ys, chunk_v, conv_s, k_s, v_s = _trunk(
        x_sample.reshape(nd * dseq, d), p_sample.reshape(DEPTH, nd * dseq, -1), W,
        nseq=nd, seq=dseq, tm=nd * dseq, sample=state)

    def kv(parts, n, s):
        return jnp.stack([a.reshape((n, s) + heads_shape) for a in parts])

    return (yp.reshape(nb, seq, d), ys.reshape(nd, dseq, d),
            kv(k_p, nb, seq), kv(v_p, nb, seq), kv(k_s, nd, dseq), kv(v_s, nd, dseq),
            jnp.stack(conv_p), jnp.stack(conv_s),
            jnp.stack([a.reshape(nd, dseq, d) for a in chunk_v]))
```

```python
import functools

import jax
import jax.numpy as jnp
from jax import lax
from jax.experimental import pallas as pl
from jax.experimental.pallas import tpu as pltpu

F32 = jnp.float32
BF16 = jnp.bfloat16

DEPTH = 4
N_MIXERS = 3
CHUNK = 128
A_GROUPS = 4
CONV_W = 3
N_HEADS = 16
ALPHA = (2 * DEPTH) ** 0.25
LN_EPS = 1e-5

LANES = 128
SUBLANES = 8
ROW_TILE = 512
ATTN_TQ = 512
ATTN_TK = 256
ATTN_UNROLL = 2
LOG2E = 1.4426950408889634
SAMPLE_PAGES = 8
NEW_PAD = 16
VMEM_LIMIT = 56 * 1024 * 1024


def _params(*semantics):
    return pltpu.CompilerParams(dimension_semantics=semantics, vmem_limit_bytes=VMEM_LIMIT)


def _resident(shape):
    zeros = (0,) * len(shape)
    return pl.BlockSpec(shape, lambda *_: zeros, pipeline_mode=pl.Buffered(1))


def _rows(tm, width):
    return pl.BlockSpec((tm, width), lambda i: (i, 0))


def _dot(a, b):
    return jnp.dot(a, b, preferred_element_type=F32)


def _layer_norm(x, g, b):
    mu = jnp.mean(x, axis=-1, keepdims=True)
    xc = x - mu
    var = jnp.mean(xc * xc, axis=-1, keepdims=True)
    return xc * lax.rsqrt(var + LN_EPS) * g + b


def _sigmoid(x):
    return 1.0 / (1.0 + jnp.exp(-x))


def _chunk_mixer_body(x_ref, w_in_ref, lng_ref, lnb_ref, ws_ref, bmap_ref, w_out_ref, mg_ref, mb_ref,
                      out_ref, *rest, emit_v):
    if emit_v:
        v_out_ref, u_ref, v_ref, s_ref = rest
    else:
        u_ref, v_ref, s_ref = rest
    tm, d = x_ref.shape
    gd = d // A_GROUPS
    x = x_ref[...]
    xb = x.astype(BF16)
    u_ref[...] = jax.nn.gelu(_dot(xb, w_in_ref[:, :d]))
    v = _layer_norm(jax.nn.gelu(_dot(xb, w_in_ref[:, d:])), lng_ref[...], lnb_ref[...])
    if emit_v:
        v_out_ref[...] = v
    v_ref[...] = v.astype(BF16)
    for c in range(tm // CHUNK):
        r0 = c * CHUNK
        for g in range(A_GROUPS):
            c0 = g * gd
            s_ref[r0:r0 + CHUNK, c0:c0 + gd] = (
                _dot(ws_ref[g], v_ref[r0:r0 + CHUNK, c0:c0 + gd]) + bmap_ref[:, c0:c0 + gd])
    gated = (u_ref[...] * s_ref[...]).astype(BF16)
    mix = _dot(gated, w_out_ref[...])
    out_ref[...] = _layer_norm(ALPHA * x + mix, mg_ref[...], mb_ref[...])


def _chunk_mixer(x, w_in, ln_g, ln_b, w_s, bmap, w_out, mg, mb, *, tm, emit_v):
    t, d = x.shape
    out_shape = [jax.ShapeDtypeStruct((t, d), F32)]
    out_specs = [_rows(tm, d)]
    if emit_v:
        out_shape.append(jax.ShapeDtypeStruct((t, d), F32))
        out_specs.append(_rows(tm, d))
    res = pl.pallas_call(
        functools.partial(_chunk_mixer_body, emit_v=emit_v),
        grid=(t // tm,),
        in_specs=[_rows(tm, d), _resident(w_in.shape), _resident(ln_g.shape), _resident(ln_b.shape),
                  _resident(w_s.shape), _resident(bmap.shape), _resident(w_out.shape),
                  _resident(mg.shape), _resident(mb.shape)],
        out_specs=out_specs,
        out_shape=out_shape,
        scratch_shapes=[pltpu.VMEM((tm, d), F32), pltpu.VMEM((tm, d), BF16), pltpu.VMEM((tm, d), F32)],
        compiler_params=_params("arbitrary"),
        name="chunk_mixer",
    )(x, w_in, ln_g, ln_b, w_s, bmap, w_out, mg, mb)
    return res if emit_v else (res[0], None)


def _conv_project(x_ref, w_in_ref, buf_ref):
    tm, d = x_ref.shape
    xb = x_ref[...].astype(BF16)
    b_gate = _dot(xb, w_in_ref[:, :d])
    xc = _dot(xb, w_in_ref[:, d:2 * d]) * _dot(xb, w_in_ref[:, 2 * d:])
    buf_ref[SUBLANES:SUBLANES + tm, :] = xc
    return b_gate, xc


def _conv_finish(x_ref, b_gate, y, w_out_ref, mg_ref, mb_ref, out_ref):
    mix = _dot((b_gate * y).astype(BF16), w_out_ref[...])
    out_ref[...] = _layer_norm(ALPHA * x_ref[...] + mix, mg_ref[...], mb_ref[...])


def _conv_prompt_body(x_ref, w_in_ref, cw_ref, w_out_ref, mg_ref, mb_ref, out_ref, tail_ref, buf_ref,
                      *, tiles_per_seq):
    tm, d = x_ref.shape
    @pl.when(pl.program_id(0) % tiles_per_seq == 0)
    def _():
        buf_ref[0:SUBLANES, :] = jnp.zeros((SUBLANES, d), F32)

    b_gate, xc = _conv_project(x_ref, w_in_ref, buf_ref)
    y = (cw_ref[0:1, :] * buf_ref[SUBLANES - 2:SUBLANES - 2 + tm, :]
         + cw_ref[1:2, :] * buf_ref[SUBLANES - 1:SUBLANES - 1 + tm, :]
         + cw_ref[2:3, :] * xc)
    _conv_finish(x_ref, b_gate, y, w_out_ref, mg_ref, mb_ref, out_ref)
    tail = buf_ref[tm:tm + SUBLANES, :]
    tail_ref[0] = tail
    buf_ref[0:SUBLANES, :] = tail


def _conv_mixer_prompt(x, w_in, cw, w_out, mg, mb, *, tm, seq):
    t, d = x.shape
    tiles_per_seq = seq // tm
    return pl.pallas_call(
        functools.partial(_conv_prompt_body, tiles_per_seq=tiles_per_seq),
        grid=(t // tm,),
        in_specs=[_rows(tm, d), _resident(w_in.shape), _resident(cw.shape), _resident(w_out.shape),
                  _resident(mg.shape), _resident(mb.shape)],
        out_specs=[_rows(tm, d), pl.BlockSpec((1, SUBLANES, d), lambda i: (i // tiles_per_seq, 0, 0))],
        out_shape=[jax.ShapeDtypeStruct((t, d), F32),
                   jax.ShapeDtypeStruct((t // seq, SUBLANES, d), F32)],
        scratch_shapes=[pltpu.VMEM((tm + SUBLANES, d), F32)],
        compiler_params=_params("arbitrary"),
        name="conv_mixer_prompt",
    )(x, w_in, cw, w_out, mg, mb)


def _conv_sample_body(x_ref, s1_ref, s2_ref, w_in_ref, cw_ref, w_out_ref, mg_ref, mb_ref,
                      out_ref, xc_ref, buf_ref, *, seq):
    tm, d = x_ref.shape
    buf_ref[0:SUBLANES, :] = jnp.zeros((SUBLANES, d), F32)
    b_gate, xc = _conv_project(x_ref, w_in_ref, buf_ref)
    xc_ref[...] = xc
    step = lax.rem(lax.broadcasted_iota(jnp.int32, (tm, 1), 0), seq)
    prev1 = jnp.where(step >= 1, buf_ref[SUBLANES - 1:SUBLANES - 1 + tm, :], s1_ref[...])
    prev2 = jnp.where(step >= 2, buf_ref[SUBLANES - 2:SUBLANES - 2 + tm, :], s2_ref[...])
    y = cw_ref[0:1, :] * prev2 + cw_ref[1:2, :] * prev1 + cw_ref[2:3, :] * xc
    _conv_finish(x_ref, b_gate, y, w_out_ref, mg_ref, mb_ref, out_ref)


def _conv_mixer_sample(x, s1, s2, w_in, cw, w_out, mg, mb, *, seq):
    t, d = x.shape
    args = (x, s1, s2, w_in, cw, w_out, mg, mb)
    return pl.pallas_call(
        functools.partial(_conv_sample_body, seq=seq),
        grid=(1,),
        in_specs=[_resident(a.shape) for a in args],
        out_specs=[_resident((t, d)), _resident((t, d))],
        out_shape=[jax.ShapeDtypeStruct((t, d), F32), jax.ShapeDtypeStruct((t, d), F32)],
        scratch_shapes=[pltpu.VMEM((t + SUBLANES, d), F32)],
        compiler_params=_params("arbitrary"),
        name="conv_mixer_sample",
    )(*args)


def _qkv_body(x_ref, w_ref, k_ref, v_ref, qb_ref, kb_ref, vb_ref, *, scale, transposed):
    d = x_ref.shape[1]
    xb = x_ref[...].astype(BF16)
    qb_ref[...] = (_dot(xb, w_ref[:, :d]) * scale).astype(BF16)
    k = _dot(xb, w_ref[:, d:2 * d])
    kb_ref[...] = k.astype(BF16)
    v = _dot(xb, w_ref[:, 2 * d:])
    vb_ref[...] = v.astype(BF16)
    if transposed:
        k_ref[0] = k.T
        v_ref[0] = v.T
    else:
        k_ref[...] = k
        v_ref[...] = v


def _qkv_project(x, w_qkv, *, tm, seq=None):
    t, d = x.shape
    scale = (d // N_HEADS) ** -0.5
    if seq is None:
        kv_shape, kv_spec = jax.ShapeDtypeStruct((t, d), F32), _rows(tm, d)
    else:
        tiles_per_seq = seq // tm
        kv_shape = jax.ShapeDtypeStruct((t // seq, d, seq), F32)
        kv_spec = pl.BlockSpec((1, d, tm), lambda i: (i // tiles_per_seq, 0, i % tiles_per_seq))
    return pl.pallas_call(
        functools.partial(_qkv_body, scale=scale, transposed=seq is not None),
        grid=(t // tm,),
        in_specs=[_rows(tm, d), _resident(w_qkv.shape)],
        out_specs=[kv_spec] * 2 + [_rows(tm, d)] * 3,
        out_shape=[kv_shape] * 2 + [jax.ShapeDtypeStruct((t, d), BF16)] * 3,
        compiler_params=_params("arbitrary"),
        name="qkv_project",
    )(x, w_qkv)


_CONTRACT_LAST = (((1,), (1,)), ((), ()))


def _suffix_ones(tk):
    row = lax.broadcasted_iota(jnp.int32, (tk, tk), 0)
    col = lax.broadcasted_iota(jnp.int32, (tk, tk), 1)
    return jnp.where(row > col, 1.0, 0.0).astype(BF16)


def _stick_scores(z, u, mask):
    sp = jnp.maximum(z, 0.0) + jnp.log(1.0 + jnp.exp2(jnp.abs(z) * -LOG2E))
    log_beta = z - sp
    if mask is not None:
        sp = jnp.where(mask, sp, 0.0)
    suffix = _dot(sp.astype(BF16), u)
    return log_beta, suffix, suffix[:, 0:1] + sp[:, 0:1]


def _stick_weights(log_beta, suffix, newer, mask):
    att = jnp.exp(log_beta - suffix - newer)
    if mask is not None:
        att = jnp.where(mask, att, 0.0)
    return att.astype(BF16)


def _attn_prompt_body(bias_ref, q_ref, k_ref, v_ref, o_ref, newer_ref, acc_ref, *, tq, tk, unroll):
    pair = pl.program_id(1)
    qi = pl.program_id(2)
    half = LANES // 2
    low = lax.broadcasted_iota(jnp.int32, (1, LANES), 1) < half
    q = q_ref[0]
    zero = jnp.zeros_like(q)
    qs = jnp.concatenate([jnp.where(low, q, zero), jnp.where(low, zero, q)], axis=0)
    row = lax.broadcasted_iota(jnp.int32, (2 * tq, 1), 0)
    first = row < tq
    bias = jnp.where(first, bias_ref[2 * pair], bias_ref[2 * pair + 1])
    q_pos = qi * tq + jnp.where(first, row, row - tq)
    u_mat = _suffix_ones(tk)

    def logits(j):
        k0 = pl.multiple_of(j * tk, tk)
        return lax.dot_general(qs, k_ref[0, pl.ds(k0, tk), :], _CONTRACT_LAST,
                               preferred_element_type=F32) + bias

    def scores(j, z, masked):
        mask = None
        if masked:
            mask = (j * tk + lax.broadcasted_iota(jnp.int32, (1, tk), 1)) < q_pos
        return _stick_scores(z, u_mat, mask) + (mask,)

    def accumulate(j, log_beta, suffix, newer, mask):
        k0 = pl.multiple_of(j * tk, tk)
        ab = _stick_weights(log_beta, suffix, newer, mask)
        vb = v_ref[0, pl.ds(k0, tk), :]
        vzero = jnp.zeros_like(vb)
        a2 = jnp.concatenate([ab[:tq], ab[tq:]], axis=1)
        v2 = jnp.concatenate([jnp.where(low, vb, vzero), jnp.where(low, vzero, vb)], axis=0)
        acc_ref[...] += _dot(a2, v2)

    def run_blocks(j_newest, zs, masked):
        halves = [scores(j_newest - i, z, masked) for i, z in enumerate(zs)]
        newer = newer_ref[...]
        for i, (log_beta, suffix, total, mask) in enumerate(halves):
            accumulate(j_newest - i, log_beta, suffix, newer, mask)
            newer = newer + total
        newer_ref[...] = newer

    newer_ref[...] = jnp.zeros_like(newer_ref)
    acc_ref[...] = jnp.zeros_like(acc_ref)
    per_tile = tq // tk
    j_diag = qi * per_tile + per_tile - 1
    run_blocks(j_diag, [logits(j_diag - i) for i in range(per_tile)], True)

    def trip(i, _):
        j_newest = qi * per_tile - 1 - unroll * i
        run_blocks(j_newest, [logits(j_newest - n) for n in range(unroll)], False)
        return 0

    lax.fori_loop(0, qi * (per_tile // unroll), trip, 0)
    o_ref[0] = acc_ref[...].astype(o_ref.dtype)


def _attn_prompt(bias, qb, kb, vb, *, tq, tk, unroll):
    nb, seq, d = qb.shape
    assert seq % tq == 0 and tq % tk == 0 and (tq // tk) % unroll == 0
    q_spec = pl.BlockSpec((1, tq, LANES), lambda b, p, i: (b, i, p))
    kv_spec = pl.BlockSpec((1, seq, LANES), lambda b, p, i: (b, 0, p))
    return pl.pallas_call(
        functools.partial(_attn_prompt_body, tq=tq, tk=tk, unroll=unroll),
        grid=(nb, d // LANES, seq // tq),
        in_specs=[pl.BlockSpec(memory_space=pltpu.SMEM), q_spec, kv_spec, kv_spec],
        out_specs=q_spec,
        out_shape=jax.ShapeDtypeStruct((nb, seq, d), BF16),
        scratch_shapes=[pltpu.VMEM((2 * tq, 1), F32), pltpu.VMEM((tq, LANES), F32)],
        compiler_params=_params("arbitrary", "arbitrary", "arbitrary"),
        name="attn_prompt",
    )(bias, qb, kb, vb)


def _attn_sample_body(pt_ref, qbd_ref, bias_ref, hm_ref, sel_ref, kn_ref, vn_ref, *rest, dec_seq, pages):
    kt_refs, vt_refs = rest[:pages], rest[pages:2 * pages]
    o_ref, newer_ref, acc_ref = rest[2 * pages:]
    j = pl.program_id(1)
    qbd = qbd_ref[0]
    m = qbd.shape[0]
    bias = bias_ref[...]

    @pl.when(j == 0)
    def _():
        step = lax.rem(lax.broadcasted_iota(jnp.int32, (m, 1), 0), dec_seq)
        mask = lax.broadcasted_iota(jnp.int32, (1, NEW_PAD), 1) < step
        z = lax.dot_general(qbd, kn_ref[0].astype(BF16), _CONTRACT_LAST,
                            preferred_element_type=F32) + bias
        log_beta, suffix, total = _stick_scores(z, _suffix_ones(NEW_PAD), mask)
        ab = _stick_weights(log_beta, suffix, jnp.zeros((m, 1), F32), mask)
        acc_ref[...] = _dot(ab, vn_ref[0].astype(BF16))
        newer_ref[...] = total

    def two_pages(refs, b):
        return jnp.concatenate([refs[2 * b + 1][0].astype(BF16), refs[2 * b][0].astype(BF16)], axis=1)

    u = _suffix_ones(2 * kt_refs[0].shape[2])
    blocks = [_stick_scores(_dot(qbd, two_pages(kt_refs, b)) + bias, u, None)
              for b in range(pages // 2)]
    newer = newer_ref[...]
    for b, (log_beta, suffix, total) in enumerate(blocks):
        ab = _stick_weights(log_beta, suffix, newer, None)
        acc_ref[...] += lax.dot_general(ab, two_pages(vt_refs, b), _CONTRACT_LAST,
                                        preferred_element_type=F32)
        newer = newer + total
    newer_ref[...] = newer

    @pl.when(j == pl.num_programs(1) - 1)
    def _():
        own = (acc_ref[...] * hm_ref[...]).astype(BF16)
        o_ref[0] = _dot(sel_ref[...], own).astype(o_ref.dtype)


def _attn_sample(page_table, qbd, bias_rows, hm_rows, sel, k_new, v_new, cache_kt, cache_vt, *, dec_seq,
                 pages):
    nseq, n_pages = page_table.shape
    _, m, d = qbd.shape
    page = cache_kt.shape[2]
    assert pages % 2 == 0 and n_pages % pages == 0
    pt_flat = page_table.reshape(-1)

    def page_spec(u):
        return pl.BlockSpec(
            (1, d, page), lambda b, j, pt: (pt[b * n_pages + n_pages - 1 - (j * pages + u)], 0, 0))

    per_seq = lambda b, j, pt: (b, 0, 0)
    const2 = lambda b, j, pt: (0, 0)
    grid_spec = pltpu.PrefetchScalarGridSpec(
        num_scalar_prefetch=1,
        grid=(nseq, n_pages // pages),
        in_specs=[pl.BlockSpec((1, m, d), per_seq),
                  pl.BlockSpec(bias_rows.shape, const2),
                  pl.BlockSpec(hm_rows.shape, const2),
                  pl.BlockSpec(sel.shape, const2),
                  pl.BlockSpec((1, NEW_PAD, d), per_seq),
                  pl.BlockSpec((1, NEW_PAD, d), per_seq)]
                 + [page_spec(u) for u in range(pages)] * 2,
        out_specs=pl.BlockSpec((1, NEW_PAD, d), per_seq),
        scratch_shapes=[pltpu.VMEM((m, 1), F32), pltpu.VMEM((m, d), F32)],
    )
    return pl.pallas_call(
        functools.partial(_attn_sample_body, dec_seq=dec_seq, pages=pages),
        grid_spec=grid_spec,
        out_shape=jax.ShapeDtypeStruct((nseq, NEW_PAD, d), BF16),
        compiler_params=_params("arbitrary", "arbitrary"),
        name="attn_sample",
    )(pt_flat, qbd, bias_rows, hm_rows, sel, k_new, v_new, *([cache_kt] * pages), *([cache_vt] * pages))


def _proj_ln_body(x_ref, o_ref, w_ref, g_ref, b_ref, out_ref):
    mix = _dot(o_ref[...], w_ref[...])
    out_ref[...] = _layer_norm(ALPHA * x_ref[...] + mix, g_ref[...], b_ref[...])


def _proj_ln(x, o, w_out, mg, mb, *, tm):
    t, d = x.shape
    return pl.pallas_call(
        _proj_ln_body,
        grid=(t // tm,),
        in_specs=[_rows(tm, d), _rows(tm, d), _resident(w_out.shape), _resident(mg.shape),
                  _resident(mb.shape)],
        out_specs=_rows(tm, d),
        out_shape=jax.ShapeDtypeStruct((t, d), F32),
        compiler_params=_params("arbitrary"),
        name="attn_out_proj",
    )(x, o, w_out, mg, mb)


def _ff_chunks(d_ff, width):
    return tuple((c0, min(c0 + width, d_ff)) for c0 in range(0, d_ff, width))


def _ffn_body(x_ref, p_ref, w_up_ref, w_down_ref, g_ref, b_ref, wg_ref, wp_ref, out_ref, acc_ref):
    d_ff = w_down_ref.shape[0]
    x = x_ref[...]
    xb = x.astype(BF16)
    for idx, (c0, c1) in enumerate(_ff_chunks(d_ff, x.shape[1])):
        gate = _dot(xb, w_up_ref[:, c0:c1])
        up = _dot(xb, w_up_ref[:, d_ff + c0:d_ff + c1])
        hidden = (gate * _sigmoid(gate) * up).astype(BF16)
        part = _dot(hidden, w_down_ref[c0:c1, :])
        if idx == 0:
            acc_ref[...] = part
        else:
            acc_ref[...] += part
    x2 = _layer_norm(ALPHA * x + acc_ref[...], g_ref[...], b_ref[...])
    ple_gate = _sigmoid(_dot(x2.astype(BF16), wg_ref[...]))
    out_ref[...] = x2 + ple_gate * _dot(p_ref[...].astype(BF16), wp_ref[...])


def _ffn_ple(x, p, w_up, w_down, g, b, w_gate, w_proj, *, tm):
    t, d = x.shape
    return pl.pallas_call(
        _ffn_body,
        grid=(t // tm,),
        in_specs=[_rows(tm, d), _rows(tm, p.shape[1]), _resident(w_up.shape), _resident(w_down.shape),
                  _resident(g.shape), _resident(b.shape), _resident(w_gate.shape),
                  _resident(w_proj.shape)],
        out_specs=_rows(tm, d),
        out_shape=jax.ShapeDtypeStruct((t, d), F32),
        scratch_shapes=[pltpu.VMEM((tm, d), F32)],
        compiler_params=_params("arbitrary"),
        name="ffn_ple",
    )(x, p, w_up, w_down, g, b, w_gate, w_proj)


def _row(vec):
    return vec.reshape(1, -1).astype(F32)


def _head_mask(d):
    head_of_lane = jnp.arange(d, dtype=jnp.int32) // (d // N_HEADS)
    return (head_of_lane[None, :] == jnp.arange(N_HEADS, dtype=jnp.int32)[:, None])


def _trunk(x, p, W, *, nseq, seq, tm, sample):
    t, d = x.shape
    chunk_v, conv_new, k_new, v_new = [], [], [], []
    for i in range(DEPTH):
        j, kind = divmod(i, N_MIXERS)
        mg, mb = _row(W["ln_mix_g"][i]), _row(W["ln_mix_b"][i])
        if kind == 0:
            w_s = jnp.where(jnp.tril(jnp.ones((CHUNK, CHUNK), bool))[None], W["a_w_s"][j], 0.0)
            b_s = W["a_b_s"][j]
            if sample is None:
                w_mix, b_rows = w_s, b_s.T
            else:
                eye = jnp.eye(t // seq, dtype=F32)
                w_mix = jax.vmap(lambda w: jnp.kron(eye, w[:seq, :seq]))(w_s)
                b_rows = jnp.tile(b_s[:, :seq].T, (t // seq, 1))
            bmap = jnp.repeat(b_rows.astype(F32), d // A_GROUPS, axis=1)
            x, v_rows = _chunk_mixer(
                x, W["a_w_in"][j].astype(BF16), _row(W["a_ln_g"][j]), _row(W["a_ln_b"][j]),
                w_mix.astype(BF16), bmap, W["a_w_out"][j].astype(BF16), mg, mb,
                tm=tm, emit_v=sample is not None)
            chunk_v.append(v_rows)
        elif kind == 1:
            w_in, cw, w_out = W["b_w_in"][j].astype(BF16), W["b_conv"][j], W["b_w_out"][j].astype(BF16)
            if sample is None:
                x, tail = _conv_mixer_prompt(x, w_in, cw, w_out, mg, mb, tm=tm, seq=seq)
                conv_new.append(tail[:, SUBLANES - (CONV_W - 1):])
            else:
                st = sample["state_conv"][j]
                zeros = jnp.zeros((nseq, seq, d), F32)
                s1 = zeros.at[:, 0].set(st[:, 1]).reshape(t, d)
                s2 = zeros.at[:, 0].set(st[:, 0]).at[:, 1].set(st[:, 1]).reshape(t, d)
                x, xc = _conv_mixer_sample(x, s1, s2, w_in, cw, w_out, mg, mb, seq=seq)
                conv_new.append(xc.reshape(nseq, seq, d)[:, seq - (CONV_W - 1):])
        else:
            heads = (N_HEADS, d // N_HEADS)
            k, v, qb, kb, vb = _qkv_project(x, W["c_w_qkv"][j].astype(BF16), tm=tm,
                                            seq=seq if sample is None else None)
            bias = W["c_b_beta"][j].astype(F32)
            if sample is None:
                k_new.append(jnp.transpose(k.reshape((nseq,) + heads + (seq,)), (0, 3, 1, 2)))
                v_new.append(jnp.transpose(v.reshape((nseq,) + heads + (seq,)), (0, 3, 1, 2)))
                o = _attn_prompt(bias, qb.reshape(nseq, seq, d), kb.reshape(nseq, seq, d),
                                 vb.reshape(nseq, seq, d), tq=ATTN_TQ, tk=ATTN_TK,
                                 unroll=ATTN_UNROLL).reshape(t, d)
            else:
                hm = _head_mask(d)
                qbd = jnp.where(hm[None, :, None, :], qb.reshape(nseq, 1, seq, d), 0)
                qbd = qbd.reshape(nseq, N_HEADS * seq, d)
                bias_rows = jnp.repeat(bias, seq)[:, None]
                hm_rows = jnp.repeat(hm.astype(F32), seq, axis=0)
                sel = (jnp.arange(NEW_PAD)[:, None] == (jnp.arange(N_HEADS * seq) % seq)[None, :])
                pad = ((0, 0), (0, NEW_PAD - seq), (0, 0))
                n_pool, page = sample["cache_k"].shape[1:3]
                transposed = lambda c: jnp.transpose(c[j], (0, 2, 3, 1)).reshape(n_pool, d, page)
                o = _attn_sample(
                    sample["page_table"], qbd, bias_rows, hm_rows, sel.astype(BF16),
                    jnp.pad(k.reshape(nseq, seq, d), pad), jnp.pad(v.reshape(nseq, seq, d), pad),
                    transposed(sample["cache_k"]), transposed(sample["cache_v"]), dec_seq=seq,
                    pages=SAMPLE_PAGES)
                o = o[:, :seq].reshape(t, d)
                k_new.append(k.reshape((nseq, seq) + heads))
                v_new.append(v.reshape((nseq, seq) + heads))
            x = _proj_ln(x, o, W["c_w_out"][j].astype(BF16), mg, mb, tm=tm)
        x = _ffn_ple(x, p[i], W["f_w_up"][i].astype(BF16), W["f_w_down"][i].astype(BF16),
                     _row(W["ln_ffn_g"][i]), _row(W["ln_ffn_b"][i]),
                     W["ple_w_gate"][i].astype(BF16), W["ple_w_proj"][i].astype(BF16), tm=tm)
    return x, chunk_v, conv_new, k_new, v_new


def kernel(x_prompt, x_sample, cache_k, cache_v, state_conv, page_table, p_prompt, p_sample, a_w_in, a_ln_g, a_ln_b, a_w_s, a_b_s, a_w_out, b_w_in, b_conv, b_w_out, c_w_qkv, c_b_beta, c_w_out, ln_mix_g, ln_mix_b, ln_ffn_g, ln_ffn_b, f_w_up, f_w_down, ple_w_gate, ple_w_proj):
    W = dict(a_w_in=a_w_in, a_ln_g=a_ln_g, a_ln_b=a_ln_b, a_w_s=a_w_s, a_b_s=a_b_s, a_w_out=a_w_out,
             b_w_in=b_w_in, b_conv=b_conv, b_w_out=b_w_out, c_w_qkv=c_w_qkv, c_b_beta=c_b_beta,
             c_w_out=c_w_out, ln_mix_g=ln_mix_g, ln_mix_b=ln_mix_b, ln_ffn_g=ln_ffn_g, ln_ffn_b=ln_ffn_b,
             f_w_up=f_w_up, f_w_down=f_w_down, ple_w_gate=ple_w_gate, ple_w_proj=ple_w_proj)
    nb, seq, d = x_prompt.shape
    nd, dseq, _ = x_sample.shape

    yp, _, conv_p, k_p, v_p = _trunk(
        x_prompt.reshape(nb * seq, d), p_prompt.reshape(DEPTH, nb * seq, -1), W,
        nseq=nb, seq=seq, tm=min(ROW_TILE, seq), sample=None)
    state = dict(state_conv=state_conv, cache_k=cache_k, cache_v=cache_v, page_table=page_table)
    ys, chunk_v, conv_s, k_s, v_s = _trunk(
        x_sample.reshape(nd * dseq, d), p_sample.reshape(DEPTH, nd * dseq, -1), W,
        nseq=nd, seq=dseq, tm=nd * dseq, sample=state)

    return (yp.reshape(nb, seq, d), ys.reshape(nd, dseq, d),
            jnp.stack(k_p), jnp.stack(v_p), jnp.stack(k_s), jnp.stack(v_s),
            jnp.stack(conv_p), jnp.stack(conv_s),
            jnp.stack([a.reshape(nd, dseq, d) for a in chunk_v]))
```

```python
import functools

import jax
import jax.numpy as jnp
from jax import lax
from jax.experimental import pallas as pl
from jax.experimental.pallas import tpu as pltpu

F32 = jnp.float32
BF16 = jnp.bfloat16

DEPTH = 4
N_MIXERS = 3
CHUNK = 128
A_GROUPS = 4
CONV_W = 3
N_HEADS = 16
ALPHA = (2 * DEPTH) ** 0.25
LN_EPS = 1e-5

LANES = 128
SUBLANES = 8
ROW_TILE = 512
MIXER_ROW_TILE = 1024
SUB_TILE = 512
ATTN_TQ = 512
ATTN_TK = 256
NO_WEIGHT = -1e30
LOG2E = 1.4426950408889634
SAMPLE_PAGES = 16
NEW_PAD = 16
VMEM_LIMIT = 56 * 1024 * 1024


def _params(*semantics):
    return pltpu.CompilerParams(dimension_semantics=semantics, vmem_limit_bytes=VMEM_LIMIT)


def _resident(shape):
    zeros = (0,) * len(shape)
    return pl.BlockSpec(shape, lambda *_: zeros, pipeline_mode=pl.Buffered(1))


def _rows(tm, width):
    return pl.BlockSpec((tm, width), lambda i: (i, 0))


def _dot(a, b):
    return jnp.dot(a, b, preferred_element_type=F32)


def _layer_norm(x, g, b):
    mu = jnp.mean(x, axis=-1, keepdims=True)
    xc = x - mu
    var = jnp.mean(xc * xc, axis=-1, keepdims=True)
    return xc * lax.rsqrt(var + LN_EPS) * g + b


def _sigmoid(x):
    return 1.0 / (1.0 + jnp.exp(-x))


def _sub_tiles(tm):
    sub = min(tm, SUB_TILE)
    assert tm % sub == 0
    return [slice(r0, r0 + sub) for r0 in range(0, tm, sub)]


def _run_staggered(stages, tiles):
    for step in range(len(stages) + len(tiles) - 1):
        for lag, rows in enumerate(tiles):
            if 0 <= step - lag < len(stages):
                stages[step - lag](rows)


def _chunk_mixer_body(x_ref, w_in_ref, lng_ref, lnb_ref, ws_ref, bmap_ref, w_out_ref, mg_ref, mb_ref,
                      out_ref, *rest, emit_v):
    if emit_v:
        v_out_ref, u_ref, v_ref, s_ref = rest
    else:
        u_ref, v_ref, s_ref = rest
    tm, d = x_ref.shape
    gd = d // A_GROUPS

    def v_matmul(rows):
        s_ref[rows, :] = _dot(x_ref[rows, :].astype(BF16), w_in_ref[:, d:])

    def v_norm(rows):
        v = _layer_norm(jax.nn.gelu(s_ref[rows, :]), lng_ref[...], lnb_ref[...])
        if emit_v:
            v_out_ref[rows, :] = v
        v_ref[rows, :] = v.astype(BF16)

    def u_matmul(rows):
        u_ref[rows, :] = _dot(x_ref[rows, :].astype(BF16), w_in_ref[:, :d])

    def u_gelu(rows):
        u_ref[rows, :] = jax.nn.gelu(u_ref[rows, :])

    def mixing(rows):
        for r0 in range(rows.start, rows.stop, CHUNK):
            for g in range(A_GROUPS):
                c0 = g * gd
                s_ref[r0:r0 + CHUNK, c0:c0 + gd] = (
                    _dot(ws_ref[g], v_ref[r0:r0 + CHUNK, c0:c0 + gd]) + bmap_ref[:, c0:c0 + gd])

    def out_matmul(rows):
        gated = (u_ref[rows, :] * s_ref[rows, :]).astype(BF16)
        s_ref[rows, :] = _dot(gated, w_out_ref[...])

    def out_norm(rows):
        out_ref[rows, :] = _layer_norm(ALPHA * x_ref[rows, :] + s_ref[rows, :], mg_ref[...], mb_ref[...])

    _run_staggered((v_matmul, v_norm, u_matmul, u_gelu, mixing, out_matmul, out_norm), _sub_tiles(tm))


def _chunk_mixer(x, w_in, ln_g, ln_b, w_s, bmap, w_out, mg, mb, *, tm, emit_v):
    t, d = x.shape
    out_shape = [jax.ShapeDtypeStruct((t, d), F32)]
    out_specs = [_rows(tm, d)]
    if emit_v:
        out_shape.append(jax.ShapeDtypeStruct((t, d), F32))
        out_specs.append(_rows(tm, d))
    res = pl.pallas_call(
        functools.partial(_chunk_mixer_body, emit_v=emit_v),
        grid=(t // tm,),
        in_specs=[_rows(tm, d), _resident(w_in.shape), _resident(ln_g.shape), _resident(ln_b.shape),
                  _resident(w_s.shape), _resident(bmap.shape), _resident(w_out.shape),
                  _resident(mg.shape), _resident(mb.shape)],
        out_specs=out_specs,
        out_shape=out_shape,
        scratch_shapes=[pltpu.VMEM((tm, d), F32), pltpu.VMEM((tm, d), BF16), pltpu.VMEM((tm, d), F32)],
        compiler_params=_params("arbitrary"),
        name="chunk_mixer",
    )(x, w_in, ln_g, ln_b, w_s, bmap, w_out, mg, mb)
    return res if emit_v else (res[0], None)


def _conv_project(x_ref, w_in_ref, buf_ref):
    tm, d = x_ref.shape
    xb = x_ref[...].astype(BF16)
    b_gate = _dot(xb, w_in_ref[:, :d])
    xc = _dot(xb, w_in_ref[:, d:2 * d]) * _dot(xb, w_in_ref[:, 2 * d:])
    buf_ref[SUBLANES:SUBLANES + tm, :] = xc
    return b_gate, xc


def _conv_finish(x_ref, b_gate, y, w_out_ref, mg_ref, mb_ref, out_ref):
    mix = _dot((b_gate * y).astype(BF16), w_out_ref[...])
    out_ref[...] = _layer_norm(ALPHA * x_ref[...] + mix, mg_ref[...], mb_ref[...])


def _conv_prompt_body(x_ref, w_in_ref, cw_ref, w_out_ref, mg_ref, mb_ref, out_ref, tail_ref, buf_ref,
                      gate_ref, *, tiles_per_seq):
    tm, d = x_ref.shape
    @pl.when(pl.program_id(0) % tiles_per_seq == 0)
    def _():
        buf_ref[0:SUBLANES, :] = jnp.zeros((SUBLANES, d), F32)

    def conv_input(rows, back):
        return buf_ref[SUBLANES - back + rows.start:SUBLANES - back + rows.stop, :]

    def project(rows):
        xb = x_ref[rows, :].astype(BF16)
        gate_ref[rows, :] = _dot(xb, w_in_ref[:, :d])
        buf_ref[SUBLANES + rows.start:SUBLANES + rows.stop, :] = (
            _dot(xb, w_in_ref[:, d:2 * d]) * _dot(xb, w_in_ref[:, 2 * d:]))

    def out_matmul(rows):
        y = (cw_ref[0:1, :] * conv_input(rows, 2) + cw_ref[1:2, :] * conv_input(rows, 1)
             + cw_ref[2:3, :] * conv_input(rows, 0))
        gate_ref[rows, :] = _dot((gate_ref[rows, :] * y).astype(BF16), w_out_ref[...])

    def out_norm(rows):
        out_ref[rows, :] = _layer_norm(ALPHA * x_ref[rows, :] + gate_ref[rows, :], mg_ref[...],
                                       mb_ref[...])

    _run_staggered((project, out_matmul, out_norm), _sub_tiles(tm))
    tail = buf_ref[tm:tm + SUBLANES, :]
    tail_ref[0] = tail
    buf_ref[0:SUBLANES, :] = tail


def _conv_mixer_prompt(x, w_in, cw, w_out, mg, mb, *, tm, seq):
    t, d = x.shape
    tiles_per_seq = seq // tm
    return pl.pallas_call(
        functools.partial(_conv_prompt_body, tiles_per_seq=tiles_per_seq),
        grid=(t // tm,),
        in_specs=[_rows(tm, d), _resident(w_in.shape), _resident(cw.shape), _resident(w_out.shape),
                  _resident(mg.shape), _resident(mb.shape)],
        out_specs=[_rows(tm, d), pl.BlockSpec((1, SUBLANES, d), lambda i: (i // tiles_per_seq, 0, 0))],
        out_shape=[jax.ShapeDtypeStruct((t, d), F32),
                   jax.ShapeDtypeStruct((t // seq, SUBLANES, d), F32)],
        scratch_shapes=[pltpu.VMEM((tm + SUBLANES, d), F32), pltpu.VMEM((tm, d), F32)],
        compiler_params=_params("arbitrary"),
        name="conv_mixer_prompt",
    )(x, w_in, cw, w_out, mg, mb)


def _conv_sample_body(x_ref, s1_ref, s2_ref, w_in_ref, cw_ref, w_out_ref, mg_ref, mb_ref,
                      out_ref, xc_ref, buf_ref, *, seq):
    tm, d = x_ref.shape
    buf_ref[0:SUBLANES, :] = jnp.zeros((SUBLANES, d), F32)
    b_gate, xc = _conv_project(x_ref, w_in_ref, buf_ref)
    xc_ref[...] = xc
    step = lax.rem(lax.broadcasted_iota(jnp.int32, (tm, 1), 0), seq)
    prev1 = jnp.where(step >= 1, buf_ref[SUBLANES - 1:SUBLANES - 1 + tm, :], s1_ref[...])
    prev2 = jnp.where(step >= 2, buf_ref[SUBLANES - 2:SUBLANES - 2 + tm, :], s2_ref[...])
    y = cw_ref[0:1, :] * prev2 + cw_ref[1:2, :] * prev1 + cw_ref[2:3, :] * xc
    _conv_finish(x_ref, b_gate, y, w_out_ref, mg_ref, mb_ref, out_ref)


def _conv_mixer_sample(x, s1, s2, w_in, cw, w_out, mg, mb, *, seq):
    t, d = x.shape
    args = (x, s1, s2, w_in, cw, w_out, mg, mb)
    return pl.pallas_call(
        functools.partial(_conv_sample_body, seq=seq),
        grid=(1,),
        in_specs=[_resident(a.shape) for a in args],
        out_specs=[_resident((t, d)), _resident((t, d))],
        out_shape=[jax.ShapeDtypeStruct((t, d), F32), jax.ShapeDtypeStruct((t, d), F32)],
        scratch_shapes=[pltpu.VMEM((t + SUBLANES, d), F32)],
        compiler_params=_params("arbitrary"),
        name="conv_mixer_sample",
    )(*args)


def _qkv_body(x_ref, w_ref, k_ref, v_ref, qb_ref, kb_ref, vb_ref, *, scale, transposed):
    d = x_ref.shape[1]
    xb = x_ref[...].astype(BF16)
    qb_ref[...] = (_dot(xb, w_ref[:, :d]) * scale).astype(BF16)
    k = _dot(xb, w_ref[:, d:2 * d])
    kb_ref[...] = k.astype(BF16)
    v = _dot(xb, w_ref[:, 2 * d:])
    vb_ref[...] = v.astype(BF16)
    if transposed:
        k_ref[0] = k.T
        v_ref[0] = v.T
    else:
        k_ref[...] = k
        v_ref[...] = v


def _qkv_project(x, w_qkv, *, tm, seq=None):
    t, d = x.shape
    scale = (d // N_HEADS) ** -0.5
    if seq is None:
        kv_shape, kv_spec = jax.ShapeDtypeStruct((t, d), F32), _rows(tm, d)
    else:
        tiles_per_seq = seq // tm
        kv_shape = jax.ShapeDtypeStruct((t // seq, d, seq), F32)
        kv_spec = pl.BlockSpec((1, d, tm), lambda i: (i // tiles_per_seq, 0, i % tiles_per_seq))
    return pl.pallas_call(
        functools.partial(_qkv_body, scale=scale, transposed=seq is not None),
        grid=(t // tm,),
        in_specs=[_rows(tm, d), _resident(w_qkv.shape)],
        out_specs=[kv_spec] * 2 + [_rows(tm, d)] * 3,
        out_shape=[kv_shape] * 2 + [jax.ShapeDtypeStruct((t, d), BF16)] * 3,
        compiler_params=_params("arbitrary"),
        name="qkv_project",
    )(x, w_qkv)


_CONTRACT_LAST = (((1,), (1,)), ((), ()))


def _suffix_ones(tk):
    row = lax.broadcasted_iota(jnp.int32, (tk, tk), 0)
    col = lax.broadcasted_iota(jnp.int32, (tk, tk), 1)
    return jnp.where(row > col, 1.0, 0.0).astype(BF16)


def _stick_scores(z, u, mask):
    sp = jnp.maximum(z, 0.0) + jnp.log(1.0 + jnp.exp2(jnp.abs(z) * -LOG2E))
    log_beta = z - sp
    if mask is not None:
        sp = jnp.where(mask, sp, 0.0)
    suffix = _dot(sp.astype(BF16), u)
    return log_beta - suffix, suffix[:, 0:1] + sp[:, 0:1]


def _stick_weights(exponent, newer, mask):
    att = jnp.exp(exponent - newer)
    if mask is not None:
        att = jnp.where(mask, att, 0.0)
    return att.astype(BF16)


def _attn_prompt_body(bias_ref, q_ref, k_ref, v_ref, o_ref, newer_ref, acc_ref, z_ref, e_ref, tot_ref,
                      *, tq, tk):
    pair = pl.program_id(1)
    qi = pl.program_id(2)
    half = LANES // 2
    low = lax.broadcasted_iota(jnp.int32, (1, LANES), 1) < half
    q = q_ref[0]
    zero = jnp.zeros_like(q)
    qs = jnp.concatenate([jnp.where(low, q, zero), jnp.where(low, zero, q)], axis=0)
    row = lax.broadcasted_iota(jnp.int32, (2 * tq, 1), 0)
    first = row < tq
    bias = jnp.where(first, bias_ref[2 * pair], bias_ref[2 * pair + 1])
    q_pos = qi * tq + jnp.where(first, row, row - tq)
    u_mat = _suffix_ones(tk)

    def logits(j):
        k0 = pl.multiple_of(j * tk, tk)
        return lax.dot_general(qs, k_ref[0, pl.ds(k0, tk), :], _CONTRACT_LAST,
                               preferred_element_type=F32) + bias

    def accumulate(j, exponent, newer, mask):
        k0 = pl.multiple_of(j * tk, tk)
        ab = _stick_weights(exponent, newer, mask)
        vb = v_ref[0, pl.ds(k0, tk), :]
        vzero = jnp.zeros_like(vb)
        a2 = jnp.concatenate([ab[:tq], ab[tq:]], axis=1)
        v2 = jnp.concatenate([jnp.where(low, vb, vzero), jnp.where(low, vzero, vb)], axis=0)
        acc_ref[...] += _dot(a2, v2)

    per_tile = tq // tk
    halves = []
    for n in range(per_tile):
        j = qi * per_tile + per_tile - 1 - n
        mask = (j * tk + lax.broadcasted_iota(jnp.int32, (1, tk), 1)) < q_pos
        halves.append((j, mask) + _stick_scores(logits(j), u_mat, mask))
    acc_ref[...] = jnp.zeros_like(acc_ref)
    newer = jnp.zeros((2 * tq, 1), F32)
    for j, mask, exponent, total in halves:
        accumulate(j, exponent, newer, mask)
        newer = newer + total
    newer_ref[...] = newer

    def stage_logits(j, slot):
        z_ref[slot] = logits(j)

    def stage_scores(slot):
        e_ref[slot], tot_ref[slot] = _stick_scores(z_ref[slot], u_mat, None)

    def stage_accumulate(j, slot):
        newer = newer_ref[...]
        accumulate(j, e_ref[slot], newer, None)
        newer_ref[...] = newer + tot_ref[slot]

    j_first = qi * per_tile - 1
    stage_logits(jnp.maximum(j_first, 0), 0)
    e_ref[1] = jnp.full(e_ref.shape[1:], NO_WEIGHT, F32)
    tot_ref[1] = jnp.zeros(tot_ref.shape[1:], F32)

    def two_steps(b):
        stage_logits(j_first - (b + 1), 1)
        stage_scores(0)
        stage_accumulate(j_first - (b - 1), 1)
        stage_logits(jnp.maximum(j_first - (b + 2), 0), 0)
        stage_scores(1)
        stage_accumulate(j_first - b, 0)

    def long_trip(i, _):
        two_steps(4 * i)
        two_steps(4 * i + 2)
        return 0

    def short_trip(i, b):
        two_steps(b)
        return b + 2

    n_pairs = qi * (per_tile // 2)
    lax.fori_loop(0, n_pairs // 2, long_trip, 0)
    lax.fori_loop(0, n_pairs % 2, short_trip, 4 * (n_pairs // 2))
    stage_accumulate(0, 1)
    o_ref[0] = acc_ref[...].astype(o_ref.dtype)


def _attn_prompt(bias, qb, kb, vb, *, tq, tk):
    nb, seq, d = qb.shape
    assert seq % tq == 0 and tq % (2 * tk) == 0
    q_spec = pl.BlockSpec((1, tq, LANES), lambda b, p, i: (b, i, p))
    kv_spec = pl.BlockSpec((1, seq, LANES), lambda b, p, i: (b, 0, p))
    return pl.pallas_call(
        functools.partial(_attn_prompt_body, tq=tq, tk=tk),
        grid=(nb, d // LANES, seq // tq),
        in_specs=[pl.BlockSpec(memory_space=pltpu.SMEM), q_spec, kv_spec, kv_spec],
        out_specs=q_spec,
        out_shape=jax.ShapeDtypeStruct((nb, seq, d), BF16),
        scratch_shapes=[pltpu.VMEM((2 * tq, 1), F32), pltpu.VMEM((tq, LANES), F32),
                        pltpu.VMEM((2, 2 * tq, tk), F32), pltpu.VMEM((2, 2 * tq, tk), F32),
                        pltpu.VMEM((2, 2 * tq, 1), F32)],
        compiler_params=_params("arbitrary", "arbitrary", "arbitrary"),
        name="attn_prompt",
    )(bias, qb, kb, vb)


def _attn_sample_body(pt_ref, qbd_ref, bias_ref, hm_ref, sel_ref, kn_ref, vn_ref, *rest, dec_seq, pages):
    kt_refs, vt_refs = rest[:pages], rest[pages:2 * pages]
    o_ref, newer_ref, acc_ref = rest[2 * pages:]
    j = pl.program_id(1)
    qbd = qbd_ref[0]
    m = qbd.shape[0]
    bias = bias_ref[...]

    @pl.when(j == 0)
    def _():
        step = lax.rem(lax.broadcasted_iota(jnp.int32, (m, 1), 0), dec_seq)
        mask = lax.broadcasted_iota(jnp.int32, (1, NEW_PAD), 1) < step
        z = lax.dot_general(qbd, kn_ref[0].astype(BF16), _CONTRACT_LAST,
                            preferred_element_type=F32) + bias
        exponent, total = _stick_scores(z, _suffix_ones(NEW_PAD), mask)
        ab = _stick_weights(exponent, jnp.zeros((m, 1), F32), mask)
        acc_ref[...] = _dot(ab, vn_ref[0].astype(BF16))
        newer_ref[...] = total

    def two_pages(refs, b):
        return jnp.concatenate([refs[2 * b + 1][0].astype(BF16), refs[2 * b][0].astype(BF16)], axis=1)

    u = _suffix_ones(2 * kt_refs[0].shape[2])
    blocks = [_stick_scores(_dot(qbd, two_pages(kt_refs, b)) + bias, u, None)
              for b in range(pages // 2)]
    newer = newer_ref[...]
    for b, (exponent, total) in enumerate(blocks):
        ab = _stick_weights(exponent, newer, None)
        acc_ref[...] += lax.dot_general(ab, two_pages(vt_refs, b), _CONTRACT_LAST,
                                        preferred_element_type=F32)
        newer = newer + total
    newer_ref[...] = newer

    @pl.when(j == pl.num_programs(1) - 1)
    def _():
        own = (acc_ref[...] * hm_ref[...]).astype(BF16)
        o_ref[0] = _dot(sel_ref[...], own).astype(o_ref.dtype)


def _attn_sample(page_table, qbd, bias_rows, hm_rows, sel, k_new, v_new, cache_kt, cache_vt, *, dec_seq,
                 pages):
    nseq, n_pages = page_table.shape
    _, m, d = qbd.shape
    page = cache_kt.shape[2]
    assert pages % 2 == 0 and n_pages % pages == 0
    pt_flat = page_table.reshape(-1)

    def page_spec(u):
        return pl.BlockSpec(
            (1, d, page), lambda b, j, pt: (pt[b * n_pages + n_pages - 1 - (j * pages + u)], 0, 0))

    per_seq = lambda b, j, pt: (b, 0, 0)
    const2 = lambda b, j, pt: (0, 0)
    grid_spec = pltpu.PrefetchScalarGridSpec(
        num_scalar_prefetch=1,
        grid=(nseq, n_pages // pages),
        in_specs=[pl.BlockSpec((1, m, d), per_seq),
                  pl.BlockSpec(bias_rows.shape, const2),
                  pl.BlockSpec(hm_rows.shape, const2),
                  pl.BlockSpec(sel.shape, const2),
                  pl.BlockSpec((1, NEW_PAD, d), per_seq),
                  pl.BlockSpec((1, NEW_PAD, d), per_seq)]
                 + [page_spec(u) for u in range(pages)] * 2,
        out_specs=pl.BlockSpec((1, NEW_PAD, d), per_seq),
        scratch_shapes=[pltpu.VMEM((m, 1), F32), pltpu.VMEM((m, d), F32)],
    )
    return pl.pallas_call(
        functools.partial(_attn_sample_body, dec_seq=dec_seq, pages=pages),
        grid_spec=grid_spec,
        out_shape=jax.ShapeDtypeStruct((nseq, NEW_PAD, d), BF16),
        compiler_params=_params("arbitrary", "arbitrary"),
        name="attn_sample",
    )(pt_flat, qbd, bias_rows, hm_rows, sel, k_new, v_new, *([cache_kt] * pages), *([cache_vt] * pages))


def _proj_ln_body(x_ref, o_ref, w_ref, g_ref, b_ref, out_ref, mix_ref):
    def matmul(rows):
        mix_ref[rows, :] = _dot(o_ref[rows, :], w_ref[...])

    def norm(rows):
        out_ref[rows, :] = _layer_norm(ALPHA * x_ref[rows, :] + mix_ref[rows, :], g_ref[...], b_ref[...])

    _run_staggered((matmul, norm), _sub_tiles(x_ref.shape[0]))


def _proj_ln(x, o, w_out, mg, mb, *, tm):
    t, d = x.shape
    return pl.pallas_call(
        _proj_ln_body,
        grid=(t // tm,),
        in_specs=[_rows(tm, d), _rows(tm, d), _resident(w_out.shape), _resident(mg.shape),
                  _resident(mb.shape)],
        out_specs=_rows(tm, d),
        out_shape=jax.ShapeDtypeStruct((t, d), F32),
        scratch_shapes=[pltpu.VMEM((tm, d), F32)],
        compiler_params=_params("arbitrary"),
        name="attn_out_proj",
    )(x, o, w_out, mg, mb)


def _ff_chunks(d_ff, width):
    return tuple((c0, min(c0 + width, d_ff)) for c0 in range(0, d_ff, width))


def _ffn_body(x_ref, p_ref, w_up_ref, w_down_ref, g_ref, b_ref, wg_ref, wp_ref, out_ref, acc_ref):
    d_ff = w_down_ref.shape[0]
    x = x_ref[...]
    xb = x.astype(BF16)
    for idx, (c0, c1) in enumerate(_ff_chunks(d_ff, x.shape[1])):
        gate = _dot(xb, w_up_ref[:, c0:c1])
        up = _dot(xb, w_up_ref[:, d_ff + c0:d_ff + c1])
        hidden = (gate * _sigmoid(gate) * up).astype(BF16)
        part = _dot(hidden, w_down_ref[c0:c1, :])
        if idx == 0:
            acc_ref[...] = part
        else:
            acc_ref[...] += part
    x2 = _layer_norm(ALPHA * x + acc_ref[...], g_ref[...], b_ref[...])
    ple_gate = _sigmoid(_dot(x2.astype(BF16), wg_ref[...]))
    out_ref[...] = x2 + ple_gate * _dot(p_ref[...].astype(BF16), wp_ref[...])


def _ffn_ple(x, p, w_up, w_down, g, b, w_gate, w_proj, *, tm):
    t, d = x.shape
    return pl.pallas_call(
        _ffn_body,
        grid=(t // tm,),
        in_specs=[_rows(tm, d), _rows(tm, p.shape[1]), _resident(w_up.shape), _resident(w_down.shape),
                  _resident(g.shape), _resident(b.shape), _resident(w_gate.shape),
                  _resident(w_proj.shape)],
        out_specs=_rows(tm, d),
        out_shape=jax.ShapeDtypeStruct((t, d), F32),
        scratch_shapes=[pltpu.VMEM((tm, d), F32)],
        compiler_params=_params("arbitrary"),
        name="ffn_ple",
    )(x, p, w_up, w_down, g, b, w_gate, w_proj)


def _row(vec):
    return vec.reshape(1, -1).astype(F32)


def _head_mask(d):
    head_of_lane = jnp.arange(d, dtype=jnp.int32) // (d // N_HEADS)
    return (head_of_lane[None, :] == jnp.arange(N_HEADS, dtype=jnp.int32)[:, None])


def _trunk(x, p, W, *, nseq, seq, tm, sample):
    t, d = x.shape
    mixer_tm = tm if sample is not None else min(MIXER_ROW_TILE, seq)
    chunk_v, conv_new, k_new, v_new = [], [], [], []
    for i in range(DEPTH):
        j, kind = divmod(i, N_MIXERS)
        mg, mb = _row(W["ln_mix_g"][i]), _row(W["ln_mix_b"][i])
        if kind == 0:
            w_s = jnp.where(jnp.tril(jnp.ones((CHUNK, CHUNK), bool))[None], W["a_w_s"][j], 0.0)
            b_s = W["a_b_s"][j]
            if sample is None:
                w_mix, b_rows = w_s, b_s.T
            else:
                eye = jnp.eye(t // seq, dtype=F32)
                w_mix = jax.vmap(lambda w: jnp.kron(eye, w[:seq, :seq]))(w_s)
                b_rows = jnp.tile(b_s[:, :seq].T, (t // seq, 1))
            bmap = jnp.repeat(b_rows.astype(F32), d // A_GROUPS, axis=1)
            x, v_rows = _chunk_mixer(
                x, W["a_w_in"][j].astype(BF16), _row(W["a_ln_g"][j]), _row(W["a_ln_b"][j]),
                w_mix.astype(BF16), bmap, W["a_w_out"][j].astype(BF16), mg, mb,
                tm=mixer_tm, emit_v=sample is not None)
            chunk_v.append(v_rows)
        elif kind == 1:
            w_in, cw, w_out = W["b_w_in"][j].astype(BF16), W["b_conv"][j], W["b_w_out"][j].astype(BF16)
            if sample is None:
                x, tail = _conv_mixer_prompt(x, w_in, cw, w_out, mg, mb, tm=mixer_tm, seq=seq)
                conv_new.append(tail[:, SUBLANES - (CONV_W - 1):])
            else:
                st = sample["state_conv"][j]
                zeros = jnp.zeros((nseq, seq, d), F32)
                s1 = zeros.at[:, 0].set(st[:, 1]).reshape(t, d)
                s2 = zeros.at[:, 0].set(st[:, 0]).at[:, 1].set(st[:, 1]).reshape(t, d)
                x, xc = _conv_mixer_sample(x, s1, s2, w_in, cw, w_out, mg, mb, seq=seq)
                conv_new.append(xc.reshape(nseq, seq, d)[:, seq - (CONV_W - 1):])
        else:
            heads = (N_HEADS, d // N_HEADS)
            k, v, qb, kb, vb = _qkv_project(x, W["c_w_qkv"][j].astype(BF16), tm=tm,
                                            seq=seq if sample is None else None)
            bias = W["c_b_beta"][j].astype(F32)
            if sample is None:
                k_new.append(jnp.transpose(k.reshape((nseq,) + heads + (seq,)), (0, 3, 1, 2)))
                v_new.append(jnp.transpose(v.reshape((nseq,) + heads + (seq,)), (0, 3, 1, 2)))
                o = _attn_prompt(bias, qb.reshape(nseq, seq, d), kb.reshape(nseq, seq, d),
                                 vb.reshape(nseq, seq, d), tq=ATTN_TQ, tk=ATTN_TK).reshape(t, d)
            else:
                hm = _head_mask(d)
                qbd = jnp.where(hm[None, :, None, :], qb.reshape(nseq, 1, seq, d), 0)
                qbd = qbd.reshape(nseq, N_HEADS * seq, d)
                bias_rows = jnp.repeat(bias, seq)[:, None]
                hm_rows = jnp.repeat(hm.astype(F32), seq, axis=0)
                sel = (jnp.arange(NEW_PAD)[:, None] == (jnp.arange(N_HEADS * seq) % seq)[None, :])
                pad = ((0, 0), (0, NEW_PAD - seq), (0, 0))
                n_pool, page = sample["cache_k"].shape[1:3]
                transposed = lambda c: jnp.transpose(c[j], (0, 2, 3, 1)).reshape(n_pool, d, page)
                o = _attn_sample(
                    sample["page_table"], qbd, bias_rows, hm_rows, sel.astype(BF16),
                    jnp.pad(k.reshape(nseq, seq, d), pad), jnp.pad(v.reshape(nseq, seq, d), pad),
                    transposed(sample["cache_k"]), transposed(sample["cache_v"]), dec_seq=seq,
                    pages=SAMPLE_PAGES)
                o = o[:, :seq].reshape(t, d)
                k_new.append(k.reshape((nseq, seq) + heads))
                v_new.append(v.reshape((nseq, seq) + heads))
            x = _proj_ln(x, o, W["c_w_out"][j].astype(BF16), mg, mb, tm=mixer_tm)
        x = _ffn_ple(x, p[i], W["f_w_up"][i].astype(BF16), W["f_w_down"][i].astype(BF16),
                     _row(W["ln_ffn_g"][i]), _row(W["ln_ffn_b"][i]),
                     W["ple_w_gate"][i].astype(BF16), W["ple_w_proj"][i].astype(BF16), tm=tm)
    return x, chunk_v, conv_new, k_new, v_new


def kernel(x_prompt, x_sample, cache_k, cache_v, state_conv, page_table, p_prompt, p_sample, a_w_in, a_ln_g, a_ln_b, a_w_s, a_b_s, a_w_out, b_w_in, b_conv, b_w_out, c_w_qkv, c_b_beta, c_w_out, ln_mix_g, ln_mix_b, ln_ffn_g, ln_ffn_b, f_w_up, f_w_down, ple_w_gate, ple_w_proj):
    W = dict(a_w_in=a_w_in, a_ln_g=a_ln_g, a_ln_b=a_ln_b, a_w_s=a_w_s, a_b_s=a_b_s, a_w_out=a_w_out,
             b_w_in=b_w_in, b_conv=b_conv, b_w_out=b_w_out, c_w_qkv=c_w_qkv, c_b_beta=c_b_beta,
             c_w_out=c_w_out, ln_mix_g=ln_mix_g, ln_mix_b=ln_mix_b, ln_ffn_g=ln_ffn_g, ln_ffn_b=ln_ffn_b,
             f_w_up=f_w_up, f_w_down=f_w_down, ple_w_gate=ple_w_gate, ple_w_proj=ple_w_proj)
    nb, seq, d = x_prompt.shape
    nd, dseq, _ = x_sample.shape

    yp, _, conv_p, k_p, v_p = _trunk(
        x_prompt.reshape(nb * seq, d), p_prompt.reshape(DEPTH, nb * seq, -1), W,
        nseq=nb, seq=seq, tm=min(ROW_TILE, seq), sample=None)
    state = dict(state_conv=state_conv, cache_k=cache_k, cache_v=cache_v, page_table=page_table)
    ys, chunk_v, conv_s, k_s, v_s = _trunk(
        x_sample.reshape(nd * dseq, d), p_sample.reshape(DEPTH, nd * dseq, -1), W,
        nseq=nd, seq=dseq, tm=nd * dseq, sample=state)

    return (yp.reshape(nb, seq, d), ys.reshape(nd, dseq, d),
            jnp.stack(k_p), jnp.stack(v_p), jnp.stack(k_s), jnp.stack(v_s),
            jnp.stack(conv_p), jnp.stack(conv_s),
            jnp.stack([a.reshape(nd, dseq, d) for a in chunk_v]))
```

```python
import functools

import jax
import jax.numpy as jnp
from jax import lax
from jax.experimental import pallas as pl
from jax.experimental.pallas import tpu as pltpu

F32 = jnp.float32
BF16 = jnp.bfloat16

DEPTH = 4
N_MIXERS = 3
CHUNK = 128
A_GROUPS = 4
CONV_W = 3
N_HEADS = 16
ALPHA = (2 * DEPTH) ** 0.25
LN_EPS = 1e-5

LANES = 128
SUBLANES = 8
ROW_TILE = 512
MIXER_ROW_TILE = 1024
SUB_TILE = 512
ATTN_TQ = 512
ATTN_TK = 256
ATTN_UNROLL = 2
LOG2E = 1.4426950408889634
SAMPLE_PAGES = 16
NEW_PAD = 16
VMEM_LIMIT = 56 * 1024 * 1024


def _params(*semantics):
    return pltpu.CompilerParams(dimension_semantics=semantics, vmem_limit_bytes=VMEM_LIMIT)


def _resident(shape):
    zeros = (0,) * len(shape)
    return pl.BlockSpec(shape, lambda *_: zeros, pipeline_mode=pl.Buffered(1))


def _rows(tm, width):
    return pl.BlockSpec((tm, width), lambda i: (i, 0))


def _dot(a, b):
    return jnp.dot(a, b, preferred_element_type=F32)


def _layer_norm(x, g, b):
    mu = jnp.mean(x, axis=-1, keepdims=True)
    xc = x - mu
    var = jnp.mean(xc * xc, axis=-1, keepdims=True)
    return xc * lax.rsqrt(var + LN_EPS) * g + b


def _sigmoid(x):
    return 1.0 / (1.0 + jnp.exp(-x))


def _sub_tiles(tm):
    sub = min(tm, SUB_TILE)
    assert tm % sub == 0
    return [slice(r0, r0 + sub) for r0 in range(0, tm, sub)]


def _run_staggered(stages, tiles):
    for step in range(len(stages) + len(tiles) - 1):
        for lag, rows in enumerate(tiles):
            if 0 <= step - lag < len(stages):
                stages[step - lag](rows)


def _chunk_mixer_body(x_ref, w_in_ref, lng_ref, lnb_ref, ws_ref, bmap_ref, w_out_ref, mg_ref, mb_ref,
                      out_ref, *rest, emit_v):
    if emit_v:
        v_out_ref, u_ref, v_ref, s_ref = rest
    else:
        u_ref, v_ref, s_ref = rest
    tm, d = x_ref.shape
    gd = d // A_GROUPS

    def v_matmul(rows):
        s_ref[rows, :] = _dot(x_ref[rows, :].astype(BF16), w_in_ref[:, d:])

    def v_norm(rows):
        v = _layer_norm(jax.nn.gelu(s_ref[rows, :]), lng_ref[...], lnb_ref[...])
        if emit_v:
            v_out_ref[rows, :] = v
        v_ref[rows, :] = v.astype(BF16)

    def u_matmul(rows):
        u_ref[rows, :] = _dot(x_ref[rows, :].astype(BF16), w_in_ref[:, :d])

    def u_gelu(rows):
        u_ref[rows, :] = jax.nn.gelu(u_ref[rows, :])

    def mixing(rows):
        for r0 in range(rows.start, rows.stop, CHUNK):
            for g in range(A_GROUPS):
                c0 = g * gd
                s_ref[r0:r0 + CHUNK, c0:c0 + gd] = (
                    _dot(ws_ref[g], v_ref[r0:r0 + CHUNK, c0:c0 + gd]) + bmap_ref[:, c0:c0 + gd])

    def out_matmul(rows):
        gated = (u_ref[rows, :] * s_ref[rows, :]).astype(BF16)
        s_ref[rows, :] = _dot(gated, w_out_ref[...])

    def out_norm(rows):
        out_ref[rows, :] = _layer_norm(ALPHA * x_ref[rows, :] + s_ref[rows, :], mg_ref[...], mb_ref[...])

    _run_staggered((v_matmul, v_norm, u_matmul, u_gelu, mixing, out_matmul, out_norm), _sub_tiles(tm))


def _chunk_mixer(x, w_in, ln_g, ln_b, w_s, bmap, w_out, mg, mb, *, tm, emit_v):
    t, d = x.shape
    out_shape = [jax.ShapeDtypeStruct((t, d), F32)]
    out_specs = [_rows(tm, d)]
    if emit_v:
        out_shape.append(jax.ShapeDtypeStruct((t, d), F32))
        out_specs.append(_rows(tm, d))
    res = pl.pallas_call(
        functools.partial(_chunk_mixer_body, emit_v=emit_v),
        grid=(t // tm,),
        in_specs=[_rows(tm, d), _resident(w_in.shape), _resident(ln_g.shape), _resident(ln_b.shape),
                  _resident(w_s.shape), _resident(bmap.shape), _resident(w_out.shape),
                  _resident(mg.shape), _resident(mb.shape)],
        out_specs=out_specs,
        out_shape=out_shape,
        scratch_shapes=[pltpu.VMEM((tm, d), F32), pltpu.VMEM((tm, d), BF16), pltpu.VMEM((tm, d), F32)],
        compiler_params=_params("arbitrary"),
        name="chunk_mixer",
    )(x, w_in, ln_g, ln_b, w_s, bmap, w_out, mg, mb)
    return res if emit_v else (res[0], None)


def _conv_project(x_ref, w_in_ref, buf_ref):
    tm, d = x_ref.shape
    xb = x_ref[...].astype(BF16)
    b_gate = _dot(xb, w_in_ref[:, :d])
    xc = _dot(xb, w_in_ref[:, d:2 * d]) * _dot(xb, w_in_ref[:, 2 * d:])
    buf_ref[SUBLANES:SUBLANES + tm, :] = xc
    return b_gate, xc


def _conv_finish(x_ref, b_gate, y, w_out_ref, mg_ref, mb_ref, out_ref):
    mix = _dot((b_gate * y).astype(BF16), w_out_ref[...])
    out_ref[...] = _layer_norm(ALPHA * x_ref[...] + mix, mg_ref[...], mb_ref[...])


def _conv_prompt_body(x_ref, w_in_ref, cw_ref, w_out_ref, mg_ref, mb_ref, out_ref, tail_ref, buf_ref,
                      gate_ref, *, tiles_per_seq):
    tm, d = x_ref.shape
    @pl.when(pl.program_id(0) % tiles_per_seq == 0)
    def _():
        buf_ref[0:SUBLANES, :] = jnp.zeros((SUBLANES, d), F32)

    def conv_input(rows, back):
        return buf_ref[SUBLANES - back + rows.start:SUBLANES - back + rows.stop, :]

    def project(rows):
        xb = x_ref[rows, :].astype(BF16)
        gate_ref[rows, :] = _dot(xb, w_in_ref[:, :d])
        buf_ref[SUBLANES + rows.start:SUBLANES + rows.stop, :] = (
            _dot(xb, w_in_ref[:, d:2 * d]) * _dot(xb, w_in_ref[:, 2 * d:]))

    def out_matmul(rows):
        y = (cw_ref[0:1, :] * conv_input(rows, 2) + cw_ref[1:2, :] * conv_input(rows, 1)
             + cw_ref[2:3, :] * conv_input(rows, 0))
        gate_ref[rows, :] = _dot((gate_ref[rows, :] * y).astype(BF16), w_out_ref[...])

    def out_norm(rows):
        out_ref[rows, :] = _layer_norm(ALPHA * x_ref[rows, :] + gate_ref[rows, :], mg_ref[...],
                                       mb_ref[...])

    _run_staggered((project, out_matmul, out_norm), _sub_tiles(tm))
    tail = buf_ref[tm:tm + SUBLANES, :]
    tail_ref[0] = tail
    buf_ref[0:SUBLANES, :] = tail


def _conv_mixer_prompt(x, w_in, cw, w_out, mg, mb, *, tm, seq):
    t, d = x.shape
    tiles_per_seq = seq // tm
    return pl.pallas_call(
        functools.partial(_conv_prompt_body, tiles_per_seq=tiles_per_seq),
        grid=(t // tm,),
        in_specs=[_rows(tm, d), _resident(w_in.shape), _resident(cw.shape), _resident(w_out.shape),
                  _resident(mg.shape), _resident(mb.shape)],
        out_specs=[_rows(tm, d), pl.BlockSpec((1, SUBLANES, d), lambda i: (i // tiles_per_seq, 0, 0))],
        out_shape=[jax.ShapeDtypeStruct((t, d), F32),
                   jax.ShapeDtypeStruct((t // seq, SUBLANES, d), F32)],
        scratch_shapes=[pltpu.VMEM((tm + SUBLANES, d), F32), pltpu.VMEM((tm, d), F32)],
        compiler_params=_params("arbitrary"),
        name="conv_mixer_prompt",
    )(x, w_in, cw, w_out, mg, mb)


def _conv_sample_body(x_ref, s1_ref, s2_ref, w_in_ref, cw_ref, w_out_ref, mg_ref, mb_ref,
                      out_ref, xc_ref, buf_ref, *, seq):
    tm, d = x_ref.shape
    buf_ref[0:SUBLANES, :] = jnp.zeros((SUBLANES, d), F32)
    b_gate, xc = _conv_project(x_ref, w_in_ref, buf_ref)
    xc_ref[...] = xc
    step = lax.rem(lax.broadcasted_iota(jnp.int32, (tm, 1), 0), seq)
    prev1 = jnp.where(step >= 1, buf_ref[SUBLANES - 1:SUBLANES - 1 + tm, :], s1_ref[...])
    prev2 = jnp.where(step >= 2, buf_ref[SUBLANES - 2:SUBLANES - 2 + tm, :], s2_ref[...])
    y = cw_ref[0:1, :] * prev2 + cw_ref[1:2, :] * prev1 + cw_ref[2:3, :] * xc
    _conv_finish(x_ref, b_gate, y, w_out_ref, mg_ref, mb_ref, out_ref)


def _conv_mixer_sample(x, s1, s2, w_in, cw, w_out, mg, mb, *, seq):
    t, d = x.shape
    args = (x, s1, s2, w_in, cw, w_out, mg, mb)
    return pl.pallas_call(
        functools.partial(_conv_sample_body, seq=seq),
        grid=(1,),
        in_specs=[_resident(a.shape) for a in args],
        out_specs=[_resident((t, d)), _resident((t, d))],
        out_shape=[jax.ShapeDtypeStruct((t, d), F32), jax.ShapeDtypeStruct((t, d), F32)],
        scratch_shapes=[pltpu.VMEM((t + SUBLANES, d), F32)],
        compiler_params=_params("arbitrary"),
        name="conv_mixer_sample",
    )(*args)


def _qkv_body(x_ref, w_ref, k_ref, v_ref, qb_ref, kb_ref, vb_ref, *, scale, transposed):
    d = x_ref.shape[1]
    xb = x_ref[...].astype(BF16)
    qb_ref[...] = (_dot(xb, w_ref[:, :d]) * scale).astype(BF16)
    k = _dot(xb, w_ref[:, d:2 * d])
    kb_ref[...] = k.astype(BF16)
    v = _dot(xb, w_ref[:, 2 * d:])
    vb_ref[...] = v.astype(BF16)
    if transposed:
        k_ref[0] = k.T
        v_ref[0] = v.T
    else:
        k_ref[...] = k
        v_ref[...] = v


def _qkv_project(x, w_qkv, *, tm, seq=None):
    t, d = x.shape
    scale = (d // N_HEADS) ** -0.5
    if seq is None:
        kv_shape, kv_spec = jax.ShapeDtypeStruct((t, d), F32), _rows(tm, d)
    else:
        tiles_per_seq = seq // tm
        kv_shape = jax.ShapeDtypeStruct((t // seq, d, seq), F32)
        kv_spec = pl.BlockSpec((1, d, tm), lambda i: (i // tiles_per_seq, 0, i % tiles_per_seq))
    return pl.pallas_call(
        functools.partial(_qkv_body, scale=scale, transposed=seq is not None),
        grid=(t // tm,),
        in_specs=[_rows(tm, d), _resident(w_qkv.shape)],
        out_specs=[kv_spec] * 2 + [_rows(tm, d)] * 3,
        out_shape=[kv_shape] * 2 + [jax.ShapeDtypeStruct((t, d), BF16)] * 3,
        compiler_params=_params("arbitrary"),
        name="qkv_project",
    )(x, w_qkv)


_CONTRACT_LAST = (((1,), (1,)), ((), ()))


def _suffix_ones(tk):
    row = lax.broadcasted_iota(jnp.int32, (tk, tk), 0)
    col = lax.broadcasted_iota(jnp.int32, (tk, tk), 1)
    return jnp.where(row > col, 1.0, 0.0).astype(BF16)


def _stick_scores(z, u, mask):
    sp = jnp.maximum(z, 0.0) + jnp.log(1.0 + jnp.exp2(jnp.abs(z) * -LOG2E))
    log_beta = z - sp
    if mask is not None:
        sp = jnp.where(mask, sp, 0.0)
    suffix = _dot(sp.astype(BF16), u)
    return log_beta - suffix, suffix[:, 0:1] + sp[:, 0:1]


def _stick_weights(exponent, newer, mask):
    att = jnp.exp(exponent - newer)
    if mask is not None:
        att = jnp.where(mask, att, 0.0)
    return att.astype(BF16)


def _attn_prompt_body(bias_ref, q_ref, k_ref, v_ref, o_ref, newer_ref, acc_ref, *, tq, tk, unroll):
    pair = pl.program_id(1)
    qi = pl.program_id(2)
    half = LANES // 2
    low = lax.broadcasted_iota(jnp.int32, (1, LANES), 1) < half
    q = q_ref[0]
    zero = jnp.zeros_like(q)
    qs = jnp.concatenate([jnp.where(low, q, zero), jnp.where(low, zero, q)], axis=0)
    row = lax.broadcasted_iota(jnp.int32, (2 * tq, 1), 0)
    first = row < tq
    bias = jnp.where(first, bias_ref[2 * pair], bias_ref[2 * pair + 1])
    q_pos = qi * tq + jnp.where(first, row, row - tq)
    u_mat = _suffix_ones(tk)

    def logits(j):
        k0 = pl.multiple_of(j * tk, tk)
        return lax.dot_general(qs, k_ref[0, pl.ds(k0, tk), :], _CONTRACT_LAST,
                               preferred_element_type=F32) + bias

    def accumulate(j, exponent, newer, mask):
        k0 = pl.multiple_of(j * tk, tk)
        ab = _stick_weights(exponent, newer, mask)
        vb = v_ref[0, pl.ds(k0, tk), :]
        vzero = jnp.zeros_like(vb)
        a2 = jnp.concatenate([ab[:tq], ab[tq:]], axis=1)
        v2 = jnp.concatenate([jnp.where(low, vb, vzero), jnp.where(low, vzero, vb)], axis=0)
        acc_ref[...] += _dot(a2, v2)

    def run_blocks(j_newest, n, masked):
        halves = []
        for i in range(n):
            j = j_newest - i
            mask = None
            if masked:
                mask = (j * tk + lax.broadcasted_iota(jnp.int32, (1, tk), 1)) < q_pos
            halves.append((j, mask) + _stick_scores(logits(j), u_mat, mask))
        newer = newer_ref[...]
        for j, mask, exponent, total in halves:
            accumulate(j, exponent, newer, mask)
            newer = newer + total
        newer_ref[...] = newer

    newer_ref[...] = jnp.zeros_like(newer_ref)
    acc_ref[...] = jnp.zeros_like(acc_ref)
    per_tile = tq // tk
    run_blocks(qi * per_tile + per_tile - 1, per_tile, True)

    def trip(i, _):
        run_blocks(qi * per_tile - 1 - unroll * i, unroll, False)
        return 0

    lax.fori_loop(0, qi * (per_tile // unroll), trip, 0)
    o_ref[0] = acc_ref[...].astype(o_ref.dtype)


def _attn_prompt(bias, qb, kb, vb, *, tq, tk, unroll):
    nb, seq, d = qb.shape
    assert seq % tq == 0 and tq % tk == 0 and (tq // tk) % unroll == 0
    q_spec = pl.BlockSpec((1, tq, LANES), lambda b, p, i: (b, i, p))
    kv_spec = pl.BlockSpec((1, seq, LANES), lambda b, p, i: (b, 0, p))
    return pl.pallas_call(
        functools.partial(_attn_prompt_body, tq=tq, tk=tk, unroll=unroll),
        grid=(nb, d // LANES, seq // tq),
        in_specs=[pl.BlockSpec(memory_space=pltpu.SMEM), q_spec, kv_spec, kv_spec],
        out_specs=q_spec,
        out_shape=jax.ShapeDtypeStruct((nb, seq, d), BF16),
        scratch_shapes=[pltpu.VMEM((2 * tq, 1), F32), pltpu.VMEM((tq, LANES), F32)],
        compiler_params=_params("arbitrary", "arbitrary", "arbitrary"),
        name="attn_prompt",
    )(bias, qb, kb, vb)


def _attn_sample_body(pt_ref, qbd_ref, bias_ref, hm_ref, sel_ref, kn_ref, vn_ref, *rest, dec_seq, pages):
    kt_refs, vt_refs = rest[:pages], rest[pages:2 * pages]
    o_ref, newer_ref, acc_ref = rest[2 * pages:]
    j = pl.program_id(1)
    qbd = qbd_ref[0]
    m = qbd.shape[0]
    bias = bias_ref[...]

    @pl.when(j == 0)
    def _():
        step = lax.rem(lax.broadcasted_iota(jnp.int32, (m, 1), 0), dec_seq)
        mask = lax.broadcasted_iota(jnp.int32, (1, NEW_PAD), 1) < step
        z = lax.dot_general(qbd, kn_ref[0].astype(BF16), _CONTRACT_LAST,
                            preferred_element_type=F32) + bias
        exponent, total = _stick_scores(z, _suffix_ones(NEW_PAD), mask)
        ab = _stick_weights(exponent, jnp.zeros((m, 1), F32), mask)
        acc_ref[...] = _dot(ab, vn_ref[0].astype(BF16))
        newer_ref[...] = total

    def two_pages(refs, b):
        return jnp.concatenate([refs[2 * b + 1][0].astype(BF16), refs[2 * b][0].astype(BF16)], axis=1)

    u = _suffix_ones(2 * kt_refs[0].shape[2])
    blocks = [_stick_scores(_dot(qbd, two_pages(kt_refs, b)) + bias, u, None)
              for b in range(pages // 2)]
    newer = newer_ref[...]
    for b, (exponent, total) in enumerate(blocks):
        ab = _stick_weights(exponent, newer, None)
        acc_ref[...] += lax.dot_general(ab, two_pages(vt_refs, b), _CONTRACT_LAST,
                                        preferred_element_type=F32)
        newer = newer + total
    newer_ref[...] = newer

    @pl.when(j == pl.num_programs(1) - 1)
    def _():
        own = (acc_ref[...] * hm_ref[...]).astype(BF16)
        o_ref[0] = _dot(sel_ref[...], own).astype(o_ref.dtype)


def _attn_sample(page_table, qbd, bias_rows, hm_rows, sel, k_new, v_new, cache_kt, cache_vt, *, dec_seq,
                 pages):
    nseq, n_pages = page_table.shape
    _, m, d = qbd.shape
    page = cache_kt.shape[2]
    assert pages % 2 == 0 and n_pages % pages == 0
    pt_flat = page_table.reshape(-1)

    def page_spec(u):
        return pl.BlockSpec(
            (1, d, page), lambda b, j, pt: (pt[b * n_pages + n_pages - 1 - (j * pages + u)], 0, 0))

    per_seq = lambda b, j, pt: (b, 0, 0)
    const2 = lambda b, j, pt: (0, 0)
    grid_spec = pltpu.PrefetchScalarGridSpec(
        num_scalar_prefetch=1,
        grid=(nseq, n_pages // pages),
        in_specs=[pl.BlockSpec((1, m, d), per_seq),
                  pl.BlockSpec(bias_rows.shape, const2),
                  pl.BlockSpec(hm_rows.shape, const2),
                  pl.BlockSpec(sel.shape, const2),
                  pl.BlockSpec((1, NEW_PAD, d), per_seq),
                  pl.BlockSpec((1, NEW_PAD, d), per_seq)]
                 + [page_spec(u) for u in range(pages)] * 2,
        out_specs=pl.BlockSpec((1, NEW_PAD, d), per_seq),
        scratch_shapes=[pltpu.VMEM((m, 1), F32), pltpu.VMEM((m, d), F32)],
    )
    return pl.pallas_call(
        functools.partial(_attn_sample_body, dec_seq=dec_seq, pages=pages),
        grid_spec=grid_spec,
        out_shape=jax.ShapeDtypeStruct((nseq, NEW_PAD, d), BF16),
        compiler_params=_params("arbitrary", "arbitrary"),
        name="attn_sample",
    )(pt_flat, qbd, bias_rows, hm_rows, sel, k_new, v_new, *([cache_kt] * pages), *([cache_vt] * pages))


def _proj_ln_body(x_ref, o_ref, w_ref, g_ref, b_ref, out_ref, mix_ref):
    def matmul(rows):
        mix_ref[rows, :] = _dot(o_ref[rows, :], w_ref[...])

    def norm(rows):
        out_ref[rows, :] = _layer_norm(ALPHA * x_ref[rows, :] + mix_ref[rows, :], g_ref[...], b_ref[...])

    _run_staggered((matmul, norm), _sub_tiles(x_ref.shape[0]))


def _proj_ln(x, o, w_out, mg, mb, *, tm):
    t, d = x.shape
    return pl.pallas_call(
        _proj_ln_body,
        grid=(t // tm,),
        in_specs=[_rows(tm, d), _rows(tm, d), _resident(w_out.shape), _resident(mg.shape),
                  _resident(mb.shape)],
        out_specs=_rows(tm, d),
        out_shape=jax.ShapeDtypeStruct((t, d), F32),
        scratch_shapes=[pltpu.VMEM((tm, d), F32)],
        compiler_params=_params("arbitrary"),
        name="attn_out_proj",
    )(x, o, w_out, mg, mb)


def _ff_chunks(d_ff, width):
    return tuple((c0, min(c0 + width, d_ff)) for c0 in range(0, d_ff, width))


def _ffn_body(x_ref, p_ref, w_up_ref, w_down_ref, g_ref, b_ref, wg_ref, wp_ref, out_ref, acc_ref):
    d_ff = w_down_ref.shape[0]
    x = x_ref[...]
    xb = x.astype(BF16)
    for idx, (c0, c1) in enumerate(_ff_chunks(d_ff, x.shape[1])):
        gate = _dot(xb, w_up_ref[:, c0:c1])
        up = _dot(xb, w_up_ref[:, d_ff + c0:d_ff + c1])
        hidden = (gate * _sigmoid(gate) * up).astype(BF16)
        part = _dot(hidden, w_down_ref[c0:c1, :])
        if idx == 0:
            acc_ref[...] = part
        else:
            acc_ref[...] += part
    x2 = _layer_norm(ALPHA * x + acc_ref[...], g_ref[...], b_ref[...])
    ple_gate = _sigmoid(_dot(x2.astype(BF16), wg_ref[...]))
    out_ref[...] = x2 + ple_gate * _dot(p_ref[...].astype(BF16), wp_ref[...])


def _ffn_ple(x, p, w_up, w_down, g, b, w_gate, w_proj, *, tm):
    t, d = x.shape
    return pl.pallas_call(
        _ffn_body,
        grid=(t // tm,),
        in_specs=[_rows(tm, d), _rows(tm, p.shape[1]), _resident(w_up.shape), _resident(w_down.shape),
                  _resident(g.shape), _resident(b.shape), _resident(w_gate.shape),
                  _resident(w_proj.shape)],
        out_specs=_rows(tm, d),
        out_shape=jax.ShapeDtypeStruct((t, d), F32),
        scratch_shapes=[pltpu.VMEM((tm, d), F32)],
        compiler_params=_params("arbitrary"),
        name="ffn_ple",
    )(x, p, w_up, w_down, g, b, w_gate, w_proj)


def _row(vec):
    return vec.reshape(1, -1).astype(F32)


def _head_mask(d):
    head_of_lane = jnp.arange(d, dtype=jnp.int32) // (d // N_HEADS)
    return (head_of_lane[None, :] == jnp.arange(N_HEADS, dtype=jnp.int32)[:, None])


def _trunk(x, p, W, *, nseq, seq, tm, sample):
    t, d = x.shape
    mixer_tm = tm if sample is not None else min(MIXER_ROW_TILE, seq)
    chunk_v, conv_new, k_new, v_new = [], [], [], []
    for i in range(DEPTH):
        j, kind = divmod(i, N_MIXERS)
        mg, mb = _row(W["ln_mix_g"][i]), _row(W["ln_mix_b"][i])
        if kind == 0:
            w_s = jnp.where(jnp.tril(jnp.ones((CHUNK, CHUNK), bool))[None], W["a_w_s"][j], 0.0)
            b_s = W["a_b_s"][j]
            if sample is None:
                w_mix, b_rows = w_s, b_s.T
            else:
                eye = jnp.eye(t // seq, dtype=F32)
                w_mix = jax.vmap(lambda w: jnp.kron(eye, w[:seq, :seq]))(w_s)
                b_rows = jnp.tile(b_s[:, :seq].T, (t // seq, 1))
            bmap = jnp.repeat(b_rows.astype(F32), d // A_GROUPS, axis=1)
            x, v_rows = _chunk_mixer(
                x, W["a_w_in"][j].astype(BF16), _row(W["a_ln_g"][j]), _row(W["a_ln_b"][j]),
                w_mix.astype(BF16), bmap, W["a_w_out"][j].astype(BF16), mg, mb,
                tm=mixer_tm, emit_v=sample is not None)
            chunk_v.append(v_rows)
        elif kind == 1:
            w_in, cw, w_out = W["b_w_in"][j].astype(BF16), W["b_conv"][j], W["b_w_out"][j].astype(BF16)
            if sample is None:
                x, tail = _conv_mixer_prompt(x, w_in, cw, w_out, mg, mb, tm=mixer_tm, seq=seq)
                conv_new.append(tail[:, SUBLANES - (CONV_W - 1):])
            else:
                st = sample["state_conv"][j]
                zeros = jnp.zeros((nseq, seq, d), F32)
                s1 = zeros.at[:, 0].set(st[:, 1]).reshape(t, d)
                s2 = zeros.at[:, 0].set(st[:, 0]).at[:, 1].set(st[:, 1]).reshape(t, d)
                x, xc = _conv_mixer_sample(x, s1, s2, w_in, cw, w_out, mg, mb, seq=seq)
                conv_new.append(xc.reshape(nseq, seq, d)[:, seq - (CONV_W - 1):])
        else:
            heads = (N_HEADS, d // N_HEADS)
            k, v, qb, kb, vb = _qkv_project(x, W["c_w_qkv"][j].astype(BF16), tm=tm,
                                            seq=seq if sample is None else None)
            bias = W["c_b_beta"][j].astype(F32)
            if sample is None:
                k_new.append(jnp.transpose(k.reshape((nseq,) + heads + (seq,)), (0, 3, 1, 2)))
                v_new.append(jnp.transpose(v.reshape((nseq,) + heads + (seq,)), (0, 3, 1, 2)))
                o = _attn_prompt(bias, qb.reshape(nseq, seq, d), kb.reshape(nseq, seq, d),
                                 vb.reshape(nseq, seq, d), tq=ATTN_TQ, tk=ATTN_TK,
                                 unroll=ATTN_UNROLL).reshape(t, d)
            else:
                hm = _head_mask(d)
                qbd = jnp.where(hm[None, :, None, :], qb.reshape(nseq, 1, seq, d), 0)
                qbd = qbd.reshape(nseq, N_HEADS * seq, d)
                bias_rows = jnp.repeat(bias, seq)[:, None]
                hm_rows = jnp.repeat(hm.astype(F32), seq, axis=0)
                sel = (jnp.arange(NEW_PAD)[:, None] == (jnp.arange(N_HEADS * seq) % seq)[None, :])
                pad = ((0, 0), (0, NEW_PAD - seq), (0, 0))
                n_pool, page = sample["cache_k"].shape[1:3]
                transposed = lambda c: jnp.transpose(c[j], (0, 2, 3, 1)).reshape(n_pool, d, page)
                o = _attn_sample(
                    sample["page_table"], qbd, bias_rows, hm_rows, sel.astype(BF16),
                    jnp.pad(k.reshape(nseq, seq, d), pad), jnp.pad(v.reshape(nseq, seq, d), pad),
                    transposed(sample["cache_k"]), transposed(sample["cache_v"]), dec_seq=seq,
                    pages=SAMPLE_PAGES)
                o = o[:, :seq].reshape(t, d)
                k_new.append(k.reshape((nseq, seq) + heads))
                v_new.append(v.reshape((nseq, seq) + heads))
            x = _proj_ln(x, o, W["c_w_out"][j].astype(BF16), mg, mb, tm=mixer_tm)
        x = _ffn_ple(x, p[i], W["f_w_up"][i].astype(BF16), W["f_w_down"][i].astype(BF16),
                     _row(W["ln_ffn_g"][i]), _row(W["ln_ffn_b"][i]),
                     W["ple_w_gate"][i].astype(BF16), W["ple_w_proj"][i].astype(BF16), tm=tm)
    return x, chunk_v, conv_new, k_new, v_new


def kernel(x_prompt, x_sample, cache_k, cache_v, state_conv, page_table, p_prompt, p_sample, a_w_in, a_ln_g, a_ln_b, a_w_s, a_b_s, a_w_out, b_w_in, b_conv, b_w_out, c_w_qkv, c_b_beta, c_w_out, ln_mix_g, ln_mix_b, ln_ffn_g, ln_ffn_b, f_w_up, f_w_down, ple_w_gate, ple_w_proj):
    W = dict(a_w_in=a_w_in, a_ln_g=a_ln_g, a_ln_b=a_ln_b, a_w_s=a_w_s, a_b_s=a_b_s, a_w_out=a_w_out,
             b_w_in=b_w_in, b_conv=b_conv, b_w_out=b_w_out, c_w_qkv=c_w_qkv, c_b_beta=c_b_beta,
             c_w_out=c_w_out, ln_mix_g=ln_mix_g, ln_mix_b=ln_mix_b, ln_ffn_g=ln_ffn_g, ln_ffn_b=ln_ffn_b,
             f_w_up=f_w_up, f_w_down=f_w_down, ple_w_gate=ple_w_gate, ple_w_proj=ple_w_proj)
    nb, seq, d = x_prompt.shape
    nd, dseq, _ = x_sample.shape

    yp, _, conv_p, k_p, v_p = _trunk(
        x_prompt.reshape(nb * seq, d), p_prompt.reshape(DEPTH, nb * seq, -1), W,
        nseq=nb, seq=seq, tm=min(ROW_TILE, seq), sample=None)
    state = dict(state_conv=state_conv, cache_k=cache_k, cache_v=cache_v, page_table=page_table)
    ys, chunk_v, conv_s, k_s, v_s = _trunk(
        x_sample.reshape(nd * dseq, d), p_sample.reshape(DEPTH, nd * dseq, -1), W,
        nseq=nd, seq=dseq, tm=nd * dseq, sample=state)

    return (yp.reshape(nb, seq, d), ys.reshape(nd, dseq, d),
            jnp.stack(k_p), jnp.stack(v_p), jnp.stack(k_s), jnp.stack(v_s),
            jnp.stack(conv_p), jnp.stack(conv_s),
            jnp.stack([a.reshape(nd, dseq, d) for a in chunk_v]))
```

```python
import functools

import jax
import jax.numpy as jnp
from jax import lax
from jax.experimental import pallas as pl
from jax.experimental.pallas import tpu as pltpu

F32 = jnp.float32
BF16 = jnp.bfloat16

DEPTH = 4
N_MIXERS = 3
CHUNK = 128
A_GROUPS = 4
CONV_W = 3
N_HEADS = 16
ALPHA = (2 * DEPTH) ** 0.25
LN_EPS = 1e-5

LANES = 128
SUBLANES = 8
ROW_TILE = 512
MIXER_ROW_TILE = 1024
SUB_TILE = 512
ATTN_TQ = 1024
ATTN_TK = 256
ATTN_UNROLL = 2
LOG2E = 1.4426950408889634
SAMPLE_PAGES = 16
NEW_PAD = 16
VMEM_LIMIT = 56 * 1024 * 1024


def _params(*semantics):
    return pltpu.CompilerParams(dimension_semantics=semantics, vmem_limit_bytes=VMEM_LIMIT)


def _resident(shape):
    zeros = (0,) * len(shape)
    return pl.BlockSpec(shape, lambda *_: zeros, pipeline_mode=pl.Buffered(1))


def _rows(tm, width):
    return pl.BlockSpec((tm, width), lambda i: (i, 0))


def _dot(a, b):
    return jnp.dot(a, b, preferred_element_type=F32)


def _layer_norm(x, g, b):
    mu = jnp.mean(x, axis=-1, keepdims=True)
    xc = x - mu
    var = jnp.mean(xc * xc, axis=-1, keepdims=True)
    return xc * lax.rsqrt(var + LN_EPS) * g + b


def _sigmoid(x):
    return 1.0 / (1.0 + jnp.exp(-x))


def _sub_tiles(tm):
    sub = min(tm, SUB_TILE)
    assert tm % sub == 0
    return [slice(r0, r0 + sub) for r0 in range(0, tm, sub)]


def _run_staggered(stages, tiles):
    for step in range(len(stages) + len(tiles) - 1):
        for lag, rows in enumerate(tiles):
            if 0 <= step - lag < len(stages):
                stages[step - lag](rows)


def _chunk_mixer_body(x_ref, w_in_ref, lng_ref, lnb_ref, ws_ref, bmap_ref, w_out_ref, mg_ref, mb_ref,
                      out_ref, *rest, emit_v):
    if emit_v:
        v_out_ref, u_ref, v_ref, s_ref = rest
    else:
        u_ref, v_ref, s_ref = rest
    tm, d = x_ref.shape
    gd = d // A_GROUPS

    def v_matmul(rows):
        s_ref[rows, :] = _dot(x_ref[rows, :].astype(BF16), w_in_ref[:, d:])

    def v_norm(rows):
        v = _layer_norm(jax.nn.gelu(s_ref[rows, :]), lng_ref[...], lnb_ref[...])
        if emit_v:
            v_out_ref[rows, :] = v
        v_ref[rows, :] = v.astype(BF16)

    def u_matmul(rows):
        u_ref[rows, :] = _dot(x_ref[rows, :].astype(BF16), w_in_ref[:, :d])

    def u_gelu(rows):
        u_ref[rows, :] = jax.nn.gelu(u_ref[rows, :])

    def mixing(rows):
        for r0 in range(rows.start, rows.stop, CHUNK):
            for g in range(A_GROUPS):
                c0 = g * gd
                s_ref[r0:r0 + CHUNK, c0:c0 + gd] = (
                    _dot(ws_ref[g], v_ref[r0:r0 + CHUNK, c0:c0 + gd]) + bmap_ref[:, c0:c0 + gd])

    def out_matmul(rows):
        gated = (u_ref[rows, :] * s_ref[rows, :]).astype(BF16)
        s_ref[rows, :] = _dot(gated, w_out_ref[...])

    def out_norm(rows):
        out_ref[rows, :] = _layer_norm(ALPHA * x_ref[rows, :] + s_ref[rows, :], mg_ref[...], mb_ref[...])

    _run_staggered((v_matmul, v_norm, u_matmul, u_gelu, mixing, out_matmul, out_norm), _sub_tiles(tm))


def _chunk_mixer(x, w_in, ln_g, ln_b, w_s, bmap, w_out, mg, mb, *, tm, emit_v):
    t, d = x.shape
    out_shape = [jax.ShapeDtypeStruct((t, d), F32)]
    out_specs = [_rows(tm, d)]
    if emit_v:
        out_shape.append(jax.ShapeDtypeStruct((t, d), F32))
        out_specs.append(_rows(tm, d))
    res = pl.pallas_call(
        functools.partial(_chunk_mixer_body, emit_v=emit_v),
        grid=(t // tm,),
        in_specs=[_rows(tm, d), _resident(w_in.shape), _resident(ln_g.shape), _resident(ln_b.shape),
                  _resident(w_s.shape), _resident(bmap.shape), _resident(w_out.shape),
                  _resident(mg.shape), _resident(mb.shape)],
        out_specs=out_specs,
        out_shape=out_shape,
        scratch_shapes=[pltpu.VMEM((tm, d), F32), pltpu.VMEM((tm, d), BF16), pltpu.VMEM((tm, d), F32)],
        compiler_params=_params("arbitrary"),
        name="chunk_mixer",
    )(x, w_in, ln_g, ln_b, w_s, bmap, w_out, mg, mb)
    return res if emit_v else (res[0], None)


def _conv_project(x_ref, w_in_ref, buf_ref):
    tm, d = x_ref.shape
    xb = x_ref[...].astype(BF16)
    b_gate = _dot(xb, w_in_ref[:, :d])
    xc = _dot(xb, w_in_ref[:, d:2 * d]) * _dot(xb, w_in_ref[:, 2 * d:])
    buf_ref[SUBLANES:SUBLANES + tm, :] = xc
    return b_gate, xc


def _conv_finish(x_ref, b_gate, y, w_out_ref, mg_ref, mb_ref, out_ref):
    mix = _dot((b_gate * y).astype(BF16), w_out_ref[...])
    out_ref[...] = _layer_norm(ALPHA * x_ref[...] + mix, mg_ref[...], mb_ref[...])


def _conv_prompt_body(x_ref, w_in_ref, cw_ref, w_out_ref, mg_ref, mb_ref, out_ref, tail_ref, buf_ref,
                      gate_ref, *, tiles_per_seq):
    tm, d = x_ref.shape
    @pl.when(pl.program_id(0) % tiles_per_seq == 0)
    def _():
        buf_ref[0:SUBLANES, :] = jnp.zeros((SUBLANES, d), F32)

    def conv_input(rows, back):
        return buf_ref[SUBLANES - back + rows.start:SUBLANES - back + rows.stop, :]

    def project(rows):
        xb = x_ref[rows, :].astype(BF16)
        gate_ref[rows, :] = _dot(xb, w_in_ref[:, :d])
        buf_ref[SUBLANES + rows.start:SUBLANES + rows.stop, :] = (
            _dot(xb, w_in_ref[:, d:2 * d]) * _dot(xb, w_in_ref[:, 2 * d:]))

    def out_matmul(rows):
        y = (cw_ref[0:1, :] * conv_input(rows, 2) + cw_ref[1:2, :] * conv_input(rows, 1)
             + cw_ref[2:3, :] * conv_input(rows, 0))
        gate_ref[rows, :] = _dot((gate_ref[rows, :] * y).astype(BF16), w_out_ref[...])

    def out_norm(rows):
        out_ref[rows, :] = _layer_norm(ALPHA * x_ref[rows, :] + gate_ref[rows, :], mg_ref[...],
                                       mb_ref[...])

    _run_staggered((project, out_matmul, out_norm), _sub_tiles(tm))
    tail = buf_ref[tm:tm + SUBLANES, :]
    tail_ref[0] = tail
    buf_ref[0:SUBLANES, :] = tail


def _conv_mixer_prompt(x, w_in, cw, w_out, mg, mb, *, tm, seq):
    t, d = x.shape
    tiles_per_seq = seq // tm
    return pl.pallas_call(
        functools.partial(_conv_prompt_body, tiles_per_seq=tiles_per_seq),
        grid=(t // tm,),
        in_specs=[_rows(tm, d), _resident(w_in.shape), _resident(cw.shape), _resident(w_out.shape),
                  _resident(mg.shape), _resident(mb.shape)],
        out_specs=[_rows(tm, d), pl.BlockSpec((1, SUBLANES, d), lambda i: (i // tiles_per_seq, 0, 0))],
        out_shape=[jax.ShapeDtypeStruct((t, d), F32),
                   jax.ShapeDtypeStruct((t // seq, SUBLANES, d), F32)],
        scratch_shapes=[pltpu.VMEM((tm + SUBLANES, d), F32), pltpu.VMEM((tm, d), F32)],
        compiler_params=_params("arbitrary"),
        name="conv_mixer_prompt",
    )(x, w_in, cw, w_out, mg, mb)


def _conv_sample_body(x_ref, s1_ref, s2_ref, w_in_ref, cw_ref, w_out_ref, mg_ref, mb_ref,
                      out_ref, xc_ref, buf_ref, *, seq):
    tm, d = x_ref.shape
    buf_ref[0:SUBLANES, :] = jnp.zeros((SUBLANES, d), F32)
    b_gate, xc = _conv_project(x_ref, w_in_ref, buf_ref)
    xc_ref[...] = xc
    step = lax.rem(lax.broadcasted_iota(jnp.int32, (tm, 1), 0), seq)
    prev1 = jnp.where(step >= 1, buf_ref[SUBLANES - 1:SUBLANES - 1 + tm, :], s1_ref[...])
    prev2 = jnp.where(step >= 2, buf_ref[SUBLANES - 2:SUBLANES - 2 + tm, :], s2_ref[...])
    y = cw_ref[0:1, :] * prev2 + cw_ref[1:2, :] * prev1 + cw_ref[2:3, :] * xc
    _conv_finish(x_ref, b_gate, y, w_out_ref, mg_ref, mb_ref, out_ref)


def _conv_mixer_sample(x, s1, s2, w_in, cw, w_out, mg, mb, *, seq):
    t, d = x.shape
    args = (x, s1, s2, w_in, cw, w_out, mg, mb)
    return pl.pallas_call(
        functools.partial(_conv_sample_body, seq=seq),
        grid=(1,),
        in_specs=[_resident(a.shape) for a in args],
        out_specs=[_resident((t, d)), _resident((t, d))],
        out_shape=[jax.ShapeDtypeStruct((t, d), F32), jax.ShapeDtypeStruct((t, d), F32)],
        scratch_shapes=[pltpu.VMEM((t + SUBLANES, d), F32)],
        compiler_params=_params("arbitrary"),
        name="conv_mixer_sample",
    )(*args)


def _qkv_body(x_ref, w_ref, k_ref, v_ref, qb_ref, kb_ref, vb_ref, *, scale, transposed):
    d = x_ref.shape[1]
    xb = x_ref[...].astype(BF16)
    qb_ref[...] = (_dot(xb, w_ref[:, :d]) * scale).astype(BF16)
    k = _dot(xb, w_ref[:, d:2 * d])
    kb_ref[...] = k.astype(BF16)
    v = _dot(xb, w_ref[:, 2 * d:])
    vb_ref[...] = v.astype(BF16)
    if transposed:
        k_ref[0] = k.T
        v_ref[0] = v.T
    else:
        k_ref[...] = k
        v_ref[...] = v


def _qkv_project(x, w_qkv, *, tm, seq=None):
    t, d = x.shape
    scale = (d // N_HEADS) ** -0.5
    if seq is None:
        kv_shape, kv_spec = jax.ShapeDtypeStruct((t, d), F32), _rows(tm, d)
    else:
        tiles_per_seq = seq // tm
        kv_shape = jax.ShapeDtypeStruct((t // seq, d, seq), F32)
        kv_spec = pl.BlockSpec((1, d, tm), lambda i: (i // tiles_per_seq, 0, i % tiles_per_seq))
    return pl.pallas_call(
        functools.partial(_qkv_body, scale=scale, transposed=seq is not None),
        grid=(t // tm,),
        in_specs=[_rows(tm, d), _resident(w_qkv.shape)],
        out_specs=[kv_spec] * 2 + [_rows(tm, d)] * 3,
        out_shape=[kv_shape] * 2 + [jax.ShapeDtypeStruct((t, d), BF16)] * 3,
        compiler_params=_params("arbitrary"),
        name="qkv_project",
    )(x, w_qkv)


_CONTRACT_LAST = (((1,), (1,)), ((), ()))


def _suffix_ones(tk):
    row = lax.broadcasted_iota(jnp.int32, (tk, tk), 0)
    col = lax.broadcasted_iota(jnp.int32, (tk, tk), 1)
    return jnp.where(row > col, 1.0, 0.0).astype(BF16)


def _stick_scores(z, u, mask):
    sp = jnp.maximum(z, 0.0) + jnp.log(1.0 + jnp.exp2(jnp.abs(z) * -LOG2E))
    log_beta = z - sp
    if mask is not None:
        sp = jnp.where(mask, sp, 0.0)
    suffix = _dot(sp.astype(BF16), u)
    return log_beta - suffix, suffix[:, 0:1] + sp[:, 0:1]


def _stick_weights(exponent, newer, mask):
    att = jnp.exp(exponent - newer)
    if mask is not None:
        att = jnp.where(mask, att, 0.0)
    return att.astype(BF16)


def _attn_prompt_body(bias_ref, q_ref, k_ref, v_ref, o_ref, newer_ref, acc_ref, *, tq, tk, unroll):
    pair = pl.program_id(1)
    qi = pl.program_id(2)
    half = LANES // 2
    low = lax.broadcasted_iota(jnp.int32, (1, LANES), 1) < half
    q = q_ref[0]
    zero = jnp.zeros_like(q)
    qs = jnp.concatenate([jnp.where(low, q, zero), jnp.where(low, zero, q)], axis=0)
    row = lax.broadcasted_iota(jnp.int32, (2 * tq, 1), 0)
    first = row < tq
    bias = jnp.where(first, bias_ref[2 * pair], bias_ref[2 * pair + 1])
    q_pos = qi * tq + jnp.where(first, row, row - tq)
    u_mat = _suffix_ones(tk)

    def from_row(x, lo):
        return x if lo == 0 else jnp.concatenate([x[lo:tq], x[tq + lo:]], axis=0)

    def logits(j, lo=0):
        k0 = pl.multiple_of(j * tk, tk)
        return lax.dot_general(from_row(qs, lo), k_ref[0, pl.ds(k0, tk), :], _CONTRACT_LAST,
                               preferred_element_type=F32) + from_row(bias, lo)

    def accumulate(j, exponent, newer, mask, lo=0):
        k0 = pl.multiple_of(j * tk, tk)
        n = tq - lo
        ab = _stick_weights(exponent, newer, mask)
        vb = v_ref[0, pl.ds(k0, tk), :]
        vzero = jnp.zeros_like(vb)
        a2 = jnp.concatenate([ab[:n], ab[n:]], axis=1)
        v2 = jnp.concatenate([jnp.where(low, vb, vzero), jnp.where(low, vzero, vb)], axis=0)
        acc_ref[lo:tq, :] += _dot(a2, v2)

    newer_ref[...] = jnp.zeros_like(newer_ref)
    acc_ref[...] = jnp.zeros_like(acc_ref)
    per_tile = tq // tk
    halves = []
    for g in reversed(range(per_tile)):
        lo = g * tk
        j = qi * per_tile + g
        mask = (j * tk + lax.broadcasted_iota(jnp.int32, (1, tk), 1)) < from_row(q_pos, lo)
        halves.append((j, lo, mask) + _stick_scores(logits(j, lo), u_mat, mask))
    for j, lo, mask, exponent, total in halves:
        newer = jnp.concatenate([newer_ref[lo:tq, :], newer_ref[tq + lo:, :]], axis=0)
        accumulate(j, exponent, newer, mask, lo)
        newer_ref[lo:tq, :] += total[:tq - lo]
        newer_ref[tq + lo:, :] += total[tq - lo:]

    def trip(i, _):
        j_newest = qi * per_tile - 1 - unroll * i
        halves = [_stick_scores(logits(j_newest - n), u_mat, None) for n in range(unroll)]
        newer = newer_ref[...]
        for n, (exponent, total) in enumerate(halves):
            accumulate(j_newest - n, exponent, newer, None)
            newer = newer + total
        newer_ref[...] = newer
        return 0

    lax.fori_loop(0, qi * (per_tile // unroll), trip, 0)
    o_ref[0] = acc_ref[...].astype(o_ref.dtype)


def _attn_prompt(bias, qb, kb, vb, *, tq, tk, unroll):
    nb, seq, d = qb.shape
    assert seq % tq == 0 and tq % tk == 0 and (tq // tk) % unroll == 0
    q_spec = pl.BlockSpec((1, tq, LANES), lambda b, p, i: (b, i, p))
    kv_spec = pl.BlockSpec((1, seq, LANES), lambda b, p, i: (b, 0, p))
    return pl.pallas_call(
        functools.partial(_attn_prompt_body, tq=tq, tk=tk, unroll=unroll),
        grid=(nb, d // LANES, seq // tq),
        in_specs=[pl.BlockSpec(memory_space=pltpu.SMEM), q_spec, kv_spec, kv_spec],
        out_specs=q_spec,
        out_shape=jax.ShapeDtypeStruct((nb, seq, d), BF16),
        scratch_shapes=[pltpu.VMEM((2 * tq, 1), F32), pltpu.VMEM((tq, LANES), F32)],
        compiler_params=_params("arbitrary", "arbitrary", "arbitrary"),
        name="attn_prompt",
    )(bias, qb, kb, vb)


def _attn_sample_body(pt_ref, qbd_ref, bias_ref, hm_ref, sel_ref, kn_ref, vn_ref, *rest, dec_seq, pages):
    kt_refs, vt_refs = rest[:pages], rest[pages:2 * pages]
    o_ref, newer_ref, acc_ref = rest[2 * pages:]
    j = pl.program_id(1)
    qbd = qbd_ref[0]
    m = qbd.shape[0]
    bias = bias_ref[...]

    @pl.when(j == 0)
    def _():
        step = lax.rem(lax.broadcasted_iota(jnp.int32, (m, 1), 0), dec_seq)
        mask = lax.broadcasted_iota(jnp.int32, (1, NEW_PAD), 1) < step
        z = lax.dot_general(qbd, kn_ref[0].astype(BF16), _CONTRACT_LAST,
                            preferred_element_type=F32) + bias
        exponent, total = _stick_scores(z, _suffix_ones(NEW_PAD), mask)
        ab = _stick_weights(exponent, jnp.zeros((m, 1), F32), mask)
        acc_ref[...] = _dot(ab, vn_ref[0].astype(BF16))
        newer_ref[...] = total

    def two_pages(refs, b):
        return jnp.concatenate([refs[2 * b + 1][0].astype(BF16), refs[2 * b][0].astype(BF16)], axis=1)

    u = _suffix_ones(2 * kt_refs[0].shape[2])
    blocks = [_stick_scores(_dot(qbd, two_pages(kt_refs, b)) + bias, u, None)
              for b in range(pages // 2)]
    newer = newer_ref[...]
    for b, (exponent, total) in enumerate(blocks):
        ab = _stick_weights(exponent, newer, None)
        acc_ref[...] += lax.dot_general(ab, two_pages(vt_refs, b), _CONTRACT_LAST,
                                        preferred_element_type=F32)
        newer = newer + total
    newer_ref[...] = newer

    @pl.when(j == pl.num_programs(1) - 1)
    def _():
        own = (acc_ref[...] * hm_ref[...]).astype(BF16)
        o_ref[0] = _dot(sel_ref[...], own).astype(o_ref.dtype)


def _attn_sample(page_table, qbd, bias_rows, hm_rows, sel, k_new, v_new, cache_kt, cache_vt, *, dec_seq,
                 pages):
    nseq, n_pages = page_table.shape
    _, m, d = qbd.shape
    page = cache_kt.shape[2]
    assert pages % 2 == 0 and n_pages % pages == 0
    pt_flat = page_table.reshape(-1)

    def page_spec(u):
        return pl.BlockSpec(
            (1, d, page), lambda b, j, pt: (pt[b * n_pages + n_pages - 1 - (j * pages + u)], 0, 0))

    per_seq = lambda b, j, pt: (b, 0, 0)
    const2 = lambda b, j, pt: (0, 0)
    grid_spec = pltpu.PrefetchScalarGridSpec(
        num_scalar_prefetch=1,
        grid=(nseq, n_pages // pages),
        in_specs=[pl.BlockSpec((1, m, d), per_seq),
                  pl.BlockSpec(bias_rows.shape, const2),
                  pl.BlockSpec(hm_rows.shape, const2),
                  pl.BlockSpec(sel.shape, const2),
                  pl.BlockSpec((1, NEW_PAD, d), per_seq),
                  pl.BlockSpec((1, NEW_PAD, d), per_seq)]
                 + [page_spec(u) for u in range(pages)] * 2,
        out_specs=pl.BlockSpec((1, NEW_PAD, d), per_seq),
        scratch_shapes=[pltpu.VMEM((m, 1), F32), pltpu.VMEM((m, d), F32)],
    )
    return pl.pallas_call(
        functools.partial(_attn_sample_body, dec_seq=dec_seq, pages=pages),
        grid_spec=grid_spec,
        out_shape=jax.ShapeDtypeStruct((nseq, NEW_PAD, d), BF16),
        compiler_params=_params("arbitrary", "arbitrary"),
        name="attn_sample",
    )(pt_flat, qbd, bias_rows, hm_rows, sel, k_new, v_new, *([cache_kt] * pages), *([cache_vt] * pages))


def _proj_ln_body(x_ref, o_ref, w_ref, g_ref, b_ref, out_ref, mix_ref):
    def matmul(rows):
        mix_ref[rows, :] = _dot(o_ref[rows, :], w_ref[...])

    def norm(rows):
        out_ref[rows, :] = _layer_norm(ALPHA * x_ref[rows, :] + mix_ref[rows, :], g_ref[...], b_ref[...])

    _run_staggered((matmul, norm), _sub_tiles(x_ref.shape[0]))


def _proj_ln(x, o, w_out, mg, mb, *, tm):
    t, d = x.shape
    return pl.pallas_call(
        _proj_ln_body,
        grid=(t // tm,),
        in_specs=[_rows(tm, d), _rows(tm, d), _resident(w_out.shape), _resident(mg.shape),
                  _resident(mb.shape)],
        out_specs=_rows(tm, d),
        out_shape=jax.ShapeDtypeStruct((t, d), F32),
        scratch_shapes=[pltpu.VMEM((tm, d), F32)],
        compiler_params=_params("arbitrary"),
        name="attn_out_proj",
    )(x, o, w_out, mg, mb)


def _ff_chunks(d_ff, width):
    return tuple((c0, min(c0 + width, d_ff)) for c0 in range(0, d_ff, width))


def _ffn_body(x_ref, p_ref, w_up_ref, w_down_ref, g_ref, b_ref, wg_ref, wp_ref, out_ref, acc_ref):
    d_ff = w_down_ref.shape[0]
    x = x_ref[...]
    xb = x.astype(BF16)
    for idx, (c0, c1) in enumerate(_ff_chunks(d_ff, x.shape[1])):
        gate = _dot(xb, w_up_ref[:, c0:c1])
        up = _dot(xb, w_up_ref[:, d_ff + c0:d_ff + c1])
        hidden = (gate * _sigmoid(gate) * up).astype(BF16)
        part = _dot(hidden, w_down_ref[c0:c1, :])
        if idx == 0:
            acc_ref[...] = part
        else:
            acc_ref[...] += part
    x2 = _layer_norm(ALPHA * x + acc_ref[...], g_ref[...], b_ref[...])
    ple_gate = _sigmoid(_dot(x2.astype(BF16), wg_ref[...]))
    out_ref[...] = x2 + ple_gate * _dot(p_ref[...].astype(BF16), wp_ref[...])


def _ffn_ple(x, p, w_up, w_down, g, b, w_gate, w_proj, *, tm):
    t, d = x.shape
    return pl.pallas_call(
        _ffn_body,
        grid=(t // tm,),
        in_specs=[_rows(tm, d), _rows(tm, p.shape[1]), _resident(w_up.shape), _resident(w_down.shape),
                  _resident(g.shape), _resident(b.shape), _resident(w_gate.shape),
                  _resident(w_proj.shape)],
        out_specs=_rows(tm, d),
        out_shape=jax.ShapeDtypeStruct((t, d), F32),
        scratch_shapes=[pltpu.VMEM((tm, d), F32)],
        compiler_params=_params("arbitrary"),
        name="ffn_ple",
    )(x, p, w_up, w_down, g, b, w_gate, w_proj)


def _row(vec):
    return vec.reshape(1, -1).astype(F32)


def _head_mask(d):
    head_of_lane = jnp.arange(d, dtype=jnp.int32) // (d // N_HEADS)
    return (head_of_lane[None, :] == jnp.arange(N_HEADS, dtype=jnp.int32)[:, None])


def _trunk(x, p, W, *, nseq, seq, tm, sample):
    t, d = x.shape
    mixer_tm = tm if sample is not None else min(MIXER_ROW_TILE, seq)
    chunk_v, conv_new, k_new, v_new = [], [], [], []
    for i in range(DEPTH):
        j, kind = divmod(i, N_MIXERS)
        mg, mb = _row(W["ln_mix_g"][i]), _row(W["ln_mix_b"][i])
        if kind == 0:
            w_s = jnp.where(jnp.tril(jnp.ones((CHUNK, CHUNK), bool))[None], W["a_w_s"][j], 0.0)
            b_s = W["a_b_s"][j]
            if sample is None:
                w_mix, b_rows = w_s, b_s.T
            else:
                eye = jnp.eye(t // seq, dtype=F32)
                w_mix = jax.vmap(lambda w: jnp.kron(eye, w[:seq, :seq]))(w_s)
                b_rows = jnp.tile(b_s[:, :seq].T, (t // seq, 1))
            bmap = jnp.repeat(b_rows.astype(F32), d // A_GROUPS, axis=1)
            x, v_rows = _chunk_mixer(
                x, W["a_w_in"][j].astype(BF16), _row(W["a_ln_g"][j]), _row(W["a_ln_b"][j]),
                w_mix.astype(BF16), bmap, W["a_w_out"][j].astype(BF16), mg, mb,
                tm=mixer_tm, emit_v=sample is not None)
            chunk_v.append(v_rows)
        elif kind == 1:
            w_in, cw, w_out = W["b_w_in"][j].astype(BF16), W["b_conv"][j], W["b_w_out"][j].astype(BF16)
            if sample is None:
                x, tail = _conv_mixer_prompt(x, w_in, cw, w_out, mg, mb, tm=mixer_tm, seq=seq)
                conv_new.append(tail[:, SUBLANES - (CONV_W - 1):])
            else:
                st = sample["state_conv"][j]
                zeros = jnp.zeros((nseq, seq, d), F32)
                s1 = zeros.at[:, 0].set(st[:, 1]).reshape(t, d)
                s2 = zeros.at[:, 0].set(st[:, 0]).at[:, 1].set(st[:, 1]).reshape(t, d)
                x, xc = _conv_mixer_sample(x, s1, s2, w_in, cw, w_out, mg, mb, seq=seq)
                conv_new.append(xc.reshape(nseq, seq, d)[:, seq - (CONV_W - 1):])
        else:
            heads = (N_HEADS, d // N_HEADS)
            k, v, qb, kb, vb = _qkv_project(x, W["c_w_qkv"][j].astype(BF16), tm=tm,
                                            seq=seq if sample is None else None)
            bias = W["c_b_beta"][j].astype(F32)
            if sample is None:
                k_new.append(jnp.transpose(k.reshape((nseq,) + heads + (seq,)), (0, 3, 1, 2)))
                v_new.append(jnp.transpose(v.reshape((nseq,) + heads + (seq,)), (0, 3, 1, 2)))
                o = _attn_prompt(bias, qb.reshape(nseq, seq, d), kb.reshape(nseq, seq, d),
                                 vb.reshape(nseq, seq, d), tq=ATTN_TQ, tk=ATTN_TK,
                                 unroll=ATTN_UNROLL).reshape(t, d)
            else:
                hm = _head_mask(d)
                qbd = jnp.where(hm[None, :, None, :], qb.reshape(nseq, 1, seq, d), 0)
                qbd = qbd.reshape(nseq, N_HEADS * seq, d)
                bias_rows = jnp.repeat(bias, seq)[:, None]
                hm_rows = jnp.repeat(hm.astype(F32), seq, axis=0)
                sel = (jnp.arange(NEW_PAD)[:, None] == (jnp.arange(N_HEADS * seq) % seq)[None, :])
                pad = ((0, 0), (0, NEW_PAD - seq), (0, 0))
                n_pool, page = sample["cache_k"].shape[1:3]
                transposed = lambda c: jnp.transpose(c[j], (0, 2, 3, 1)).reshape(n_pool, d, page)
                o = _attn_sample(
                    sample["page_table"], qbd, bias_rows, hm_rows, sel.astype(BF16),
                    jnp.pad(k.reshape(nseq, seq, d), pad), jnp.pad(v.reshape(nseq, seq, d), pad),
                    transposed(sample["cache_k"]), transposed(sample["cache_v"]), dec_seq=seq,
                    pages=SAMPLE_PAGES)
                o = o[:, :seq].reshape(t, d)
                k_new.append(k.reshape((nseq, seq) + heads))
                v_new.append(v.reshape((nseq, seq) + heads))
            x = _proj_ln(x, o, W["c_w_out"][j].astype(BF16), mg, mb, tm=mixer_tm)
        x = _ffn_ple(x, p[i], W["f_w_up"][i].astype(BF16), W["f_w_down"][i].astype(BF16),
                     _row(W["ln_ffn_g"][i]), _row(W["ln_ffn_b"][i]),
                     W["ple_w_gate"][i].astype(BF16), W["ple_w_proj"][i].astype(BF16), tm=tm)
    return x, chunk_v, conv_new, k_new, v_new


def kernel(x_prompt, x_sample, cache_k, cache_v, state_conv, page_table, p_prompt, p_sample, a_w_in, a_ln_g, a_ln_b, a_w_s, a_b_s, a_w_out, b_w_in, b_conv, b_w_out, c_w_qkv, c_b_beta, c_w_out, ln_mix_g, ln_mix_b, ln_ffn_g, ln_ffn_b, f_w_up, f_w_down, ple_w_gate, ple_w_proj):
    W = dict(a_w_in=a_w_in, a_ln_g=a_ln_g, a_ln_b=a_ln_b, a_w_s=a_w_s, a_b_s=a_b_s, a_w_out=a_w_out,
             b_w_in=b_w_in, b_conv=b_conv, b_w_out=b_w_out, c_w_qkv=c_w_qkv, c_b_beta=c_b_beta,
             c_w_out=c_w_out, ln_mix_g=ln_mix_g, ln_mix_b=ln_mix_b, ln_ffn_g=ln_ffn_g, ln_ffn_b=ln_ffn_b,
             f_w_up=f_w_up, f_w_down=f_w_down, ple_w_gate=ple_w_gate, ple_w_proj=ple_w_proj)
    nb, seq, d = x_prompt.shape
    nd, dseq, _ = x_sample.shape

    yp, _, conv_p, k_p, v_p = _trunk(
        x_prompt.reshape(nb * seq, d), p_prompt.reshape(DEPTH, nb * seq, -1), W,
        nseq=nb, seq=seq, tm=min(ROW_TILE, seq), sample=None)
    state = dict(state_conv=state_conv, cache_k=cache_k, cache_v=cache_v, page_table=page_table)
    ys, chunk_v, conv_s, k_s, v_s = _trunk(
        x_sample.reshape(nd * dseq, d), p_sample.reshape(DEPTH, nd * dseq, -1), W,
        nseq=nd, seq=dseq, tm=nd * dseq, sample=state)

    return (yp.reshape(nb, seq, d), ys.reshape(nd, dseq, d),
            jnp.stack(k_p), jnp.stack(v_p), jnp.stack(k_s), jnp.stack(v_s),
            jnp.stack(conv_p), jnp.stack(conv_s),
            jnp.stack([a.reshape(nd, dseq, d) for a in chunk_v]))
```

```python
import functools

import jax
import jax.numpy as jnp
from jax import lax
from jax.experimental import pallas as pl
from jax.experimental.pallas import tpu as pltpu

F32 = jnp.float32
BF16 = jnp.bfloat16

DEPTH = 4
N_MIXERS = 3
CHUNK = 128
A_GROUPS = 4
CONV_W = 3
N_HEADS = 16
ALPHA = (2 * DEPTH) ** 0.25
LN_EPS = 1e-5

LANES = 128
SUBLANES = 8
ROW_TILE = 512
MIXER_ROW_TILE = 1024
SUB_TILE = 512
ATTN_TQ = 1024
ATTN_TK = 256
ATTN_UNROLL = 2
LOG2E = 1.4426950408889634
SAMPLE_PAGES = 16
NEW_PAD = 16
VMEM_LIMIT = 56 * 1024 * 1024


def _params(*semantics):
    return pltpu.CompilerParams(dimension_semantics=semantics, vmem_limit_bytes=VMEM_LIMIT)


def _resident(shape):
    zeros = (0,) * len(shape)
    return pl.BlockSpec(shape, lambda *_: zeros, pipeline_mode=pl.Buffered(1))


def _rows(tm, width):
    return pl.BlockSpec((tm, width), lambda i: (i, 0))


def _dot(a, b):
    return jnp.dot(a, b, preferred_element_type=F32)


def _layer_norm(x, g, b):
    mu = jnp.mean(x, axis=-1, keepdims=True)
    xc = x - mu
    var = jnp.mean(xc * xc, axis=-1, keepdims=True)
    return xc * lax.rsqrt(var + LN_EPS) * g + b


def _sigmoid(x):
    return 1.0 / (1.0 + jnp.exp(-x))


def _sub_tiles(tm):
    sub = min(tm, SUB_TILE)
    assert tm % sub == 0
    return [slice(r0, r0 + sub) for r0 in range(0, tm, sub)]


def _run_staggered(stages, tiles):
    for step in range(len(stages) + len(tiles) - 1):
        for lag, rows in enumerate(tiles):
            if 0 <= step - lag < len(stages):
                stages[step - lag](rows)


def _chunk_mixer_body(x_ref, w_in_ref, lng_ref, lnb_ref, ws_ref, bmap_ref, w_out_ref, mg_ref, mb_ref,
                      out_ref, *rest, emit_v):
    if emit_v:
        v_out_ref, u_ref, v_ref, s_ref = rest
    else:
        u_ref, v_ref, s_ref = rest
    tm, d = x_ref.shape
    gd = d // A_GROUPS

    def v_matmul(rows):
        s_ref[rows, :] = _dot(x_ref[rows, :].astype(BF16), w_in_ref[:, d:])

    def v_norm(rows):
        v = _layer_norm(jax.nn.gelu(s_ref[rows, :]), lng_ref[...], lnb_ref[...])
        if emit_v:
            v_out_ref[rows, :] = v
        v_ref[rows, :] = v.astype(BF16)

    def u_matmul(rows):
        u_ref[rows, :] = _dot(x_ref[rows, :].astype(BF16), w_in_ref[:, :d])

    def u_gelu(rows):
        u_ref[rows, :] = jax.nn.gelu(u_ref[rows, :])

    def mixing(rows):
        for r0 in range(rows.start, rows.stop, CHUNK):
            for g in range(A_GROUPS):
                c0 = g * gd
                s_ref[r0:r0 + CHUNK, c0:c0 + gd] = (
                    _dot(ws_ref[g], v_ref[r0:r0 + CHUNK, c0:c0 + gd]) + bmap_ref[:, c0:c0 + gd])

    def out_matmul(rows):
        gated = (u_ref[rows, :] * s_ref[rows, :]).astype(BF16)
        s_ref[rows, :] = _dot(gated, w_out_ref[...])

    def out_norm(rows):
        out_ref[rows, :] = _layer_norm(ALPHA * x_ref[rows, :] + s_ref[rows, :], mg_ref[...], mb_ref[...])

    _run_staggered((v_matmul, v_norm, u_matmul, u_gelu, mixing, out_matmul, out_norm), _sub_tiles(tm))


def _chunk_mixer(x, w_in, ln_g, ln_b, w_s, bmap, w_out, mg, mb, *, tm, emit_v):
    t, d = x.shape
    out_shape = [jax.ShapeDtypeStruct((t, d), F32)]
    out_specs = [_rows(tm, d)]
    if emit_v:
        out_shape.append(jax.ShapeDtypeStruct((t, d), F32))
        out_specs.append(_rows(tm, d))
    res = pl.pallas_call(
        functools.partial(_chunk_mixer_body, emit_v=emit_v),
        grid=(t // tm,),
        in_specs=[_rows(tm, d), _resident(w_in.shape), _resident(ln_g.shape), _resident(ln_b.shape),
                  _resident(w_s.shape), _resident(bmap.shape), _resident(w_out.shape),
                  _resident(mg.shape), _resident(mb.shape)],
        out_specs=out_specs,
        out_shape=out_shape,
        scratch_shapes=[pltpu.VMEM((tm, d), F32), pltpu.VMEM((tm, d), BF16), pltpu.VMEM((tm, d), F32)],
        compiler_params=_params("arbitrary"),
        name="chunk_mixer",
    )(x, w_in, ln_g, ln_b, w_s, bmap, w_out, mg, mb)
    return res if emit_v else (res[0], None)


def _conv_project(x_ref, w_in_ref, buf_ref):
    tm, d = x_ref.shape
    xb = x_ref[...].astype(BF16)
    b_gate = _dot(xb, w_in_ref[:, :d])
    xc = _dot(xb, w_in_ref[:, d:2 * d]) * _dot(xb, w_in_ref[:, 2 * d:])
    buf_ref[SUBLANES:SUBLANES + tm, :] = xc
    return b_gate, xc


def _conv_finish(x_ref, b_gate, y, w_out_ref, mg_ref, mb_ref, out_ref):
    mix = _dot((b_gate * y).astype(BF16), w_out_ref[...])
    out_ref[...] = _layer_norm(ALPHA * x_ref[...] + mix, mg_ref[...], mb_ref[...])


def _conv_prompt_body(x_ref, w_in_ref, cw_ref, w_out_ref, mg_ref, mb_ref, out_ref, tail_ref, buf_ref,
                      gate_ref, *, tiles_per_seq):
    tm, d = x_ref.shape
    @pl.when(pl.program_id(0) % tiles_per_seq == 0)
    def _():
        buf_ref[0:SUBLANES, :] = jnp.zeros((SUBLANES, d), F32)

    def conv_input(rows, back):
        return buf_ref[SUBLANES - back + rows.start:SUBLANES - back + rows.stop, :]

    def project(rows):
        xb = x_ref[rows, :].astype(BF16)
        gate_ref[rows, :] = _dot(xb, w_in_ref[:, :d])
        buf_ref[SUBLANES + rows.start:SUBLANES + rows.stop, :] = (
            _dot(xb, w_in_ref[:, d:2 * d]) * _dot(xb, w_in_ref[:, 2 * d:]))

    def out_matmul(rows):
        y = (cw_ref[0:1, :] * conv_input(rows, 2) + cw_ref[1:2, :] * conv_input(rows, 1)
             + cw_ref[2:3, :] * conv_input(rows, 0))
        gate_ref[rows, :] = _dot((gate_ref[rows, :] * y).astype(BF16), w_out_ref[...])

    def out_norm(rows):
        out_ref[rows, :] = _layer_norm(ALPHA * x_ref[rows, :] + gate_ref[rows, :], mg_ref[...],
                                       mb_ref[...])

    _run_staggered((project, out_matmul, out_norm), _sub_tiles(tm))
    tail = buf_ref[tm:tm + SUBLANES, :]
    tail_ref[0] = tail
    buf_ref[0:SUBLANES, :] = tail


def _conv_mixer_prompt(x, w_in, cw, w_out, mg, mb, *, tm, seq):
    t, d = x.shape
    tiles_per_seq = seq // tm
    return pl.pallas_call(
        functools.partial(_conv_prompt_body, tiles_per_seq=tiles_per_seq),
        grid=(t // tm,),
        in_specs=[_rows(tm, d), _resident(w_in.shape), _resident(cw.shape), _resident(w_out.shape),
                  _resident(mg.shape), _resident(mb.shape)],
        out_specs=[_rows(tm, d), pl.BlockSpec((1, SUBLANES, d), lambda i: (i // tiles_per_seq, 0, 0))],
        out_shape=[jax.ShapeDtypeStruct((t, d), F32),
                   jax.ShapeDtypeStruct((t // seq, SUBLANES, d), F32)],
        scratch_shapes=[pltpu.VMEM((tm + SUBLANES, d), F32), pltpu.VMEM((tm, d), F32)],
        compiler_params=_params("arbitrary"),
        name="conv_mixer_prompt",
    )(x, w_in, cw, w_out, mg, mb)


def _conv_sample_body(x_ref, s1_ref, s2_ref, w_in_ref, cw_ref, w_out_ref, mg_ref, mb_ref,
                      out_ref, xc_ref, buf_ref, *, seq):
    tm, d = x_ref.shape
    buf_ref[0:SUBLANES, :] = jnp.zeros((SUBLANES, d), F32)
    b_gate, xc = _conv_project(x_ref, w_in_ref, buf_ref)
    xc_ref[...] = xc
    step = lax.rem(lax.broadcasted_iota(jnp.int32, (tm, 1), 0), seq)
    prev1 = jnp.where(step >= 1, buf_ref[SUBLANES - 1:SUBLANES - 1 + tm, :], s1_ref[...])
    prev2 = jnp.where(step >= 2, buf_ref[SUBLANES - 2:SUBLANES - 2 + tm, :], s2_ref[...])
    y = cw_ref[0:1, :] * prev2 + cw_ref[1:2, :] * prev1 + cw_ref[2:3, :] * xc
    _conv_finish(x_ref, b_gate, y, w_out_ref, mg_ref, mb_ref, out_ref)


def _conv_mixer_sample(x, s1, s2, w_in, cw, w_out, mg, mb, *, seq):
    t, d = x.shape
    args = (x, s1, s2, w_in, cw, w_out, mg, mb)
    return pl.pallas_call(
        functools.partial(_conv_sample_body, seq=seq),
        grid=(1,),
        in_specs=[_resident(a.shape) for a in args],
        out_specs=[_resident((t, d)), _resident((t, d))],
        out_shape=[jax.ShapeDtypeStruct((t, d), F32), jax.ShapeDtypeStruct((t, d), F32)],
        scratch_shapes=[pltpu.VMEM((t + SUBLANES, d), F32)],
        compiler_params=_params("arbitrary"),
        name="conv_mixer_sample",
    )(*args)


def _qkv_body(x_ref, w_ref, k_ref, v_ref, qb_ref, kb_ref, vb_ref, *, scale, transposed):
    d = x_ref.shape[1]
    xb = x_ref[...].astype(BF16)
    qb_ref[...] = (_dot(xb, w_ref[:, :d]) * scale).astype(BF16)
    k = _dot(xb, w_ref[:, d:2 * d])
    kb_ref[...] = k.astype(BF16)
    v = _dot(xb, w_ref[:, 2 * d:])
    vb_ref[...] = v.astype(BF16)
    if transposed:
        k_ref[0] = k.T
        v_ref[0] = v.T
    else:
        k_ref[...] = k
        v_ref[...] = v


def _qkv_project(x, w_qkv, *, tm, seq=None):
    t, d = x.shape
    scale = (d // N_HEADS) ** -0.5
    if seq is None:
        kv_shape, kv_spec = jax.ShapeDtypeStruct((t, d), F32), _rows(tm, d)
    else:
        tiles_per_seq = seq // tm
        kv_shape = jax.ShapeDtypeStruct((t // seq, d, seq), F32)
        kv_spec = pl.BlockSpec((1, d, tm), lambda i: (i // tiles_per_seq, 0, i % tiles_per_seq))
    return pl.pallas_call(
        functools.partial(_qkv_body, scale=scale, transposed=seq is not None),
        grid=(t // tm,),
        in_specs=[_rows(tm, d), _resident(w_qkv.shape)],
        out_specs=[kv_spec] * 2 + [_rows(tm, d)] * 3,
        out_shape=[kv_shape] * 2 + [jax.ShapeDtypeStruct((t, d), BF16)] * 3,
        compiler_params=_params("arbitrary"),
        name="qkv_project",
    )(x, w_qkv)


_CONTRACT_LAST = (((1,), (1,)), ((), ()))


def _suffix_ones(tk):
    row = lax.broadcasted_iota(jnp.int32, (tk, tk), 0)
    col = lax.broadcasted_iota(jnp.int32, (tk, tk), 1)
    return jnp.where(row > col, 1.0, 0.0).astype(BF16)


def _stick_scores(z, u, mask):
    sp = jnp.maximum(z, 0.0) + jnp.log(1.0 + jnp.exp2(jnp.abs(z) * -LOG2E))
    log_beta = z - sp
    if mask is not None:
        sp = jnp.where(mask, sp, 0.0)
    suffix = _dot(sp.astype(BF16), u)
    return log_beta - suffix, suffix[:, 0:1] + sp[:, 0:1]


def _stick_weights(exponent, newer, mask):
    att = jnp.exp((exponent - newer).astype(BF16))
    if mask is not None:
        att = jnp.where(mask, att, jnp.zeros_like(att))
    return att


def _attn_prompt_body(bias_ref, q_ref, k_ref, v_ref, o_ref, newer_ref, acc_ref, *, tq, tk, unroll):
    pair = pl.program_id(1)
    qi = pl.program_id(2)
    half = LANES // 2
    low = lax.broadcasted_iota(jnp.int32, (1, LANES), 1) < half
    q = q_ref[0]
    zero = jnp.zeros_like(q)
    qs = jnp.concatenate([jnp.where(low, q, zero), jnp.where(low, zero, q)], axis=0)
    row = lax.broadcasted_iota(jnp.int32, (2 * tq, 1), 0)
    first = row < tq
    bias = jnp.where(first, bias_ref[2 * pair], bias_ref[2 * pair + 1])
    q_pos = qi * tq + jnp.where(first, row, row - tq)
    u_mat = _suffix_ones(tk)

    def from_row(x, lo):
        return x if lo == 0 else jnp.concatenate([x[lo:tq], x[tq + lo:]], axis=0)

    def logits(j, lo=0):
        k0 = pl.multiple_of(j * tk, tk)
        return lax.dot_general(from_row(qs, lo), k_ref[0, pl.ds(k0, tk), :], _CONTRACT_LAST,
                               preferred_element_type=F32) + from_row(bias, lo)

    def accumulate(j, exponent, newer, mask, lo=0):
        k0 = pl.multiple_of(j * tk, tk)
        n = tq - lo
        ab = _stick_weights(exponent, newer, mask)
        vb = v_ref[0, pl.ds(k0, tk), :]
        vzero = jnp.zeros_like(vb)
        a2 = jnp.concatenate([ab[:n], ab[n:]], axis=1)
        v2 = jnp.concatenate([jnp.where(low, vb, vzero), jnp.where(low, vzero, vb)], axis=0)
        acc_ref[lo:tq, :] += _dot(a2, v2)

    newer_ref[...] = jnp.zeros_like(newer_ref)
    acc_ref[...] = jnp.zeros_like(acc_ref)
    per_tile = tq // tk
    halves = []
    for g in reversed(range(per_tile)):
        lo = g * tk
        j = qi * per_tile + g
        mask = (j * tk + lax.broadcasted_iota(jnp.int32, (1, tk), 1)) < from_row(q_pos, lo)
        halves.append((j, lo, mask) + _stick_scores(logits(j, lo), u_mat, mask))
    for j, lo, mask, exponent, total in halves:
        newer = jnp.concatenate([newer_ref[lo:tq, :], newer_ref[tq + lo:, :]], axis=0)
        accumulate(j, exponent, newer, mask, lo)
        newer_ref[lo:tq, :] += total[:tq - lo]
        newer_ref[tq + lo:, :] += total[tq - lo:]

    def trip(i, _):
        j_newest = qi * per_tile - 1 - unroll * i
        halves = [_stick_scores(logits(j_newest - n), u_mat, None) for n in range(unroll)]
        newer = newer_ref[...]
        for n, (exponent, total) in enumerate(halves):
            accumulate(j_newest - n, exponent, newer, None)
            newer = newer + total
        newer_ref[...] = newer
        return 0

    lax.fori_loop(0, qi * (per_tile // unroll), trip, 0)
    o_ref[0] = acc_ref[...].astype(o_ref.dtype)


def _attn_prompt(bias, qb, kb, vb, *, tq, tk, unroll):
    nb, seq, d = qb.shape
    assert seq % tq == 0 and tq % tk == 0 and (tq // tk) % unroll == 0
    q_spec = pl.BlockSpec((1, tq, LANES), lambda b, p, i: (b, i, p))
    kv_spec = pl.BlockSpec((1, seq, LANES), lambda b, p, i: (b, 0, p))
    return pl.pallas_call(
        functools.partial(_attn_prompt_body, tq=tq, tk=tk, unroll=unroll),
        grid=(nb, d // LANES, seq // tq),
        in_specs=[pl.BlockSpec(memory_space=pltpu.SMEM), q_spec, kv_spec, kv_spec],
        out_specs=q_spec,
        out_shape=jax.ShapeDtypeStruct((nb, seq, d), BF16),
        scratch_shapes=[pltpu.VMEM((2 * tq, 1), F32), pltpu.VMEM((tq, LANES), F32)],
        compiler_params=_params("arbitrary", "arbitrary", "arbitrary"),
        name="attn_prompt",
    )(bias, qb, kb, vb)


def _attn_sample_body(pt_ref, qbd_ref, bias_ref, hm_ref, sel_ref, kn_ref, vn_ref, *rest, dec_seq, pages):
    kt_refs, vt_refs = rest[:pages], rest[pages:2 * pages]
    o_ref, newer_ref, acc_ref = rest[2 * pages:]
    j = pl.program_id(1)
    qbd = qbd_ref[0]
    m = qbd.shape[0]
    bias = bias_ref[...]

    @pl.when(j == 0)
    def _():
        step = lax.rem(lax.broadcasted_iota(jnp.int32, (m, 1), 0), dec_seq)
        mask = lax.broadcasted_iota(jnp.int32, (1, NEW_PAD), 1) < step
        z = lax.dot_general(qbd, kn_ref[0].astype(BF16), _CONTRACT_LAST,
                            preferred_element_type=F32) + bias
        exponent, total = _stick_scores(z, _suffix_ones(NEW_PAD), mask)
        ab = _stick_weights(exponent, jnp.zeros((m, 1), F32), mask)
        acc_ref[...] = _dot(ab, vn_ref[0].astype(BF16))
        newer_ref[...] = total

    def two_pages(refs, b):
        return jnp.concatenate([refs[2 * b + 1][0].astype(BF16), refs[2 * b][0].astype(BF16)], axis=1)

    u = _suffix_ones(2 * kt_refs[0].shape[2])
    blocks = [_stick_scores(_dot(qbd, two_pages(kt_refs, b)) + bias, u, None)
              for b in range(pages // 2)]
    newer = newer_ref[...]
    for b, (exponent, total) in enumerate(blocks):
        ab = _stick_weights(exponent, newer, None)
        acc_ref[...] += lax.dot_general(ab, two_pages(vt_refs, b), _CONTRACT_LAST,
                                        preferred_element_type=F32)
        newer = newer + total
    newer_ref[...] = newer

    @pl.when(j == pl.num_programs(1) - 1)
    def _():
        own = (acc_ref[...] * hm_ref[...]).astype(BF16)
        o_ref[0] = _dot(sel_ref[...], own).astype(o_ref.dtype)


def _attn_sample(page_table, qbd, bias_rows, hm_rows, sel, k_new, v_new, cache_kt, cache_vt, *, dec_seq,
                 pages):
    nseq, n_pages = page_table.shape
    _, m, d = qbd.shape
    page = cache_kt.shape[2]
    assert pages % 2 == 0 and n_pages % pages == 0
    pt_flat = page_table.reshape(-1)

    def page_spec(u):
        return pl.BlockSpec(
            (1, d, page), lambda b, j, pt: (pt[b * n_pages + n_pages - 1 - (j * pages + u)], 0, 0))

    per_seq = lambda b, j, pt: (b, 0, 0)
    const2 = lambda b, j, pt: (0, 0)
    grid_spec = pltpu.PrefetchScalarGridSpec(
        num_scalar_prefetch=1,
        grid=(nseq, n_pages // pages),
        in_specs=[pl.BlockSpec((1, m, d), per_seq),
                  pl.BlockSpec(bias_rows.shape, const2),
                  pl.BlockSpec(hm_rows.shape, const2),
                  pl.BlockSpec(sel.shape, const2),
                  pl.BlockSpec((1, NEW_PAD, d), per_seq),
                  pl.BlockSpec((1, NEW_PAD, d), per_seq)]
                 + [page_spec(u) for u in range(pages)] * 2,
        out_specs=pl.BlockSpec((1, NEW_PAD, d), per_seq),
        scratch_shapes=[pltpu.VMEM((m, 1), F32), pltpu.VMEM((m, d), F32)],
    )
    return pl.pallas_call(
        functools.partial(_attn_sample_body, dec_seq=dec_seq, pages=pages),
        grid_spec=grid_spec,
        out_shape=jax.ShapeDtypeStruct((nseq, NEW_PAD, d), BF16),
        compiler_params=_params("arbitrary", "arbitrary"),
        name="attn_sample",
    )(pt_flat, qbd, bias_rows, hm_rows, sel, k_new, v_new, *([cache_kt] * pages), *([cache_vt] * pages))


def _proj_ln_body(x_ref, o_ref, w_ref, g_ref, b_ref, out_ref, mix_ref):
    def matmul(rows):
        mix_ref[rows, :] = _dot(o_ref[rows, :], w_ref[...])

    def norm(rows):
        out_ref[rows, :] = _layer_norm(ALPHA * x_ref[rows, :] + mix_ref[rows, :], g_ref[...], b_ref[...])

    _run_staggered((matmul, norm), _sub_tiles(x_ref.shape[0]))


def _proj_ln(x, o, w_out, mg, mb, *, tm):
    t, d = x.shape
    return pl.pallas_call(
        _proj_ln_body,
        grid=(t // tm,),
        in_specs=[_rows(tm, d), _rows(tm, d), _resident(w_out.shape), _resident(mg.shape),
                  _resident(mb.shape)],
        out_specs=_rows(tm, d),
        out_shape=jax.ShapeDtypeStruct((t, d), F32),
        scratch_shapes=[pltpu.VMEM((tm, d), F32)],
        compiler_params=_params("arbitrary"),
        name="attn_out_proj",
    )(x, o, w_out, mg, mb)


def _ff_chunks(d_ff, width):
    return tuple((c0, min(c0 + width, d_ff)) for c0 in range(0, d_ff, width))


def _ffn_body(x_ref, p_ref, w_up_ref, w_down_ref, g_ref, b_ref, wg_ref, wp_ref, out_ref, acc_ref):
    d_ff = w_down_ref.shape[0]
    x = x_ref[...]
    xb = x.astype(BF16)
    for idx, (c0, c1) in enumerate(_ff_chunks(d_ff, x.shape[1])):
        gate = _dot(xb, w_up_ref[:, c0:c1])
        up = _dot(xb, w_up_ref[:, d_ff + c0:d_ff + c1])
        hidden = (gate * _sigmoid(gate) * up).astype(BF16)
        part = _dot(hidden, w_down_ref[c0:c1, :])
        if idx == 0:
            acc_ref[...] = part
        else:
            acc_ref[...] += part
    x2 = _layer_norm(ALPHA * x + acc_ref[...], g_ref[...], b_ref[...])
    ple_gate = _sigmoid(_dot(x2.astype(BF16), wg_ref[...]))
    out_ref[...] = x2 + ple_gate * _dot(p_ref[...].astype(BF16), wp_ref[...])


def _ffn_ple(x, p, w_up, w_down, g, b, w_gate, w_proj, *, tm):
    t, d = x.shape
    return pl.pallas_call(
        _ffn_body,
        grid=(t // tm,),
        in_specs=[_rows(tm, d), _rows(tm, p.shape[1]), _resident(w_up.shape), _resident(w_down.shape),
                  _resident(g.shape), _resident(b.shape), _resident(w_gate.shape),
                  _resident(w_proj.shape)],
        out_specs=_rows(tm, d),
        out_shape=jax.ShapeDtypeStruct((t, d), F32),
        scratch_shapes=[pltpu.VMEM((tm, d), F32)],
        compiler_params=_params("arbitrary"),
        name="ffn_ple",
    )(x, p, w_up, w_down, g, b, w_gate, w_proj)


def _row(vec):
    return vec.reshape(1, -1).astype(F32)


def _head_mask(d):
    head_of_lane = jnp.arange(d, dtype=jnp.int32) // (d // N_HEADS)
    return (head_of_lane[None, :] == jnp.arange(N_HEADS, dtype=jnp.int32)[:, None])


def _trunk(x, p, W, *, nseq, seq, tm, sample):
    t, d = x.shape
    mixer_tm = tm if sample is not None else min(MIXER_ROW_TILE, seq)
    chunk_v, conv_new, k_new, v_new = [], [], [], []
    for i in range(DEPTH):
        j, kind = divmod(i, N_MIXERS)
        mg, mb = _row(W["ln_mix_g"][i]), _row(W["ln_mix_b"][i])
        if kind == 0:
            w_s = jnp.where(jnp.tril(jnp.ones((CHUNK, CHUNK), bool))[None], W["a_w_s"][j], 0.0)
            b_s = W["a_b_s"][j]
            if sample is None:
                w_mix, b_rows = w_s, b_s.T
            else:
                eye = jnp.eye(t // seq, dtype=F32)
                w_mix = jax.vmap(lambda w: jnp.kron(eye, w[:seq, :seq]))(w_s)
                b_rows = jnp.tile(b_s[:, :seq].T, (t // seq, 1))
            bmap = jnp.repeat(b_rows.astype(F32), d // A_GROUPS, axis=1)
            x, v_rows = _chunk_mixer(
                x, W["a_w_in"][j].astype(BF16), _row(W["a_ln_g"][j]), _row(W["a_ln_b"][j]),
                w_mix.astype(BF16), bmap, W["a_w_out"][j].astype(BF16), mg, mb,
                tm=mixer_tm, emit_v=sample is not None)
            chunk_v.append(v_rows)
        elif kind == 1:
            w_in, cw, w_out = W["b_w_in"][j].astype(BF16), W["b_conv"][j], W["b_w_out"][j].astype(BF16)
            if sample is None:
                x, tail = _conv_mixer_prompt(x, w_in, cw, w_out, mg, mb, tm=mixer_tm, seq=seq)
                conv_new.append(tail[:, SUBLANES - (CONV_W - 1):])
            else:
                st = sample["state_conv"][j]
                zeros = jnp.zeros((nseq, seq, d), F32)
                s1 = zeros.at[:, 0].set(st[:, 1]).reshape(t, d)
                s2 = zeros.at[:, 0].set(st[:, 0]).at[:, 1].set(st[:, 1]).reshape(t, d)
                x, xc = _conv_mixer_sample(x, s1, s2, w_in, cw, w_out, mg, mb, seq=seq)
                conv_new.append(xc.reshape(nseq, seq, d)[:, seq - (CONV_W - 1):])
        else:
            heads = (N_HEADS, d // N_HEADS)
            k, v, qb, kb, vb = _qkv_project(x, W["c_w_qkv"][j].astype(BF16), tm=tm,
                                            seq=seq if sample is None else None)
            bias = W["c_b_beta"][j].astype(F32)
            if sample is None:
                k_new.append(jnp.transpose(k.reshape((nseq,) + heads + (seq,)), (0, 3, 1, 2)))
                v_new.append(jnp.transpose(v.reshape((nseq,) + heads + (seq,)), (0, 3, 1, 2)))
                o = _attn_prompt(bias, qb.reshape(nseq, seq, d), kb.reshape(nseq, seq, d),
                                 vb.reshape(nseq, seq, d), tq=ATTN_TQ, tk=ATTN_TK,
                                 unroll=ATTN_UNROLL).reshape(t, d)
            else:
                hm = _head_mask(d)
                qbd = jnp.where(hm[None, :, None, :], qb.reshape(nseq, 1, seq, d), 0)
                qbd = qbd.reshape(nseq, N_HEADS * seq, d)
                bias_rows = jnp.repeat(bias, seq)[:, None]
                hm_rows = jnp.repeat(hm.astype(F32), seq, axis=0)
                sel = (jnp.arange(NEW_PAD)[:, None] == (jnp.arange(N_HEADS * seq) % seq)[None, :])
                pad = ((0, 0), (0, NEW_PAD - seq), (0, 0))
                n_pool, page = sample["cache_k"].shape[1:3]
                transposed = lambda c: jnp.transpose(c[j], (0, 2, 3, 1)).reshape(n_pool, d, page)
                o = _attn_sample(
                    sample["page_table"], qbd, bias_rows, hm_rows, sel.astype(BF16),
                    jnp.pad(k.reshape(nseq, seq, d), pad), jnp.pad(v.reshape(nseq, seq, d), pad),
                    transposed(sample["cache_k"]), transposed(sample["cache_v"]), dec_seq=seq,
                    pages=SAMPLE_PAGES)
                o = o[:, :seq].reshape(t, d)
                k_new.append(k.reshape((nseq, seq) + heads))
                v_new.append(v.reshape((nseq, seq) + heads))
            x = _proj_ln(x, o, W["c_w_out"][j].astype(BF16), mg, mb, tm=mixer_tm)
        x = _ffn_ple(x, p[i], W["f_w_up"][i].astype(BF16), W["f_w_down"][i].astype(BF16),
                     _row(W["ln_ffn_g"][i]), _row(W["ln_ffn_b"][i]),
                     W["ple_w_gate"][i].astype(BF16), W["ple_w_proj"][i].astype(BF16), tm=tm)
    return x, chunk_v, conv_new, k_new, v_new


def kernel(x_prompt, x_sample, cache_k, cache_v, state_conv, page_table, p_prompt, p_sample, a_w_in, a_ln_g, a_ln_b, a_w_s, a_b_s, a_w_out, b_w_in, b_conv, b_w_out, c_w_qkv, c_b_beta, c_w_out, ln_mix_g, ln_mix_b, ln_ffn_g, ln_ffn_b, f_w_up, f_w_down, ple_w_gate, ple_w_proj):
    W = dict(a_w_in=a_w_in, a_ln_g=a_ln_g, a_ln_b=a_ln_b, a_w_s=a_w_s, a_b_s=a_b_s, a_w_out=a_w_out,
             b_w_in=b_w_in, b_conv=b_conv, b_w_out=b_w_out, c_w_qkv=c_w_qkv, c_b_beta=c_b_beta,
             c_w_out=c_w_out, ln_mix_g=ln_mix_g, ln_mix_b=ln_mix_b, ln_ffn_g=ln_ffn_g, ln_ffn_b=ln_ffn_b,
             f_w_up=f_w_up, f_w_down=f_w_down, ple_w_gate=ple_w_gate, ple_w_proj=ple_w_proj)
    nb, seq, d = x_prompt.shape
    nd, dseq, _ = x_sample.shape

    yp, _, conv_p, k_p, v_p = _trunk(
        x_prompt.reshape(nb * seq, d), p_prompt.reshape(DEPTH, nb * seq, -1), W,
        nseq=nb, seq=seq, tm=min(ROW_TILE, seq), sample=None)
    state = dict(state_conv=state_conv, cache_k=cache_k, cache_v=cache_v, page_table=page_table)
    ys, chunk_v, conv_s, k_s, v_s = _trunk(
        x_sample.reshape(nd * dseq, d), p_sample.reshape(DEPTH, nd * dseq, -1), W,
        nseq=nd, seq=dseq, tm=nd * dseq, sample=state)

    return (yp.reshape(nb, seq, d), ys.reshape(nd, dseq, d),
            jnp.stack(k_p), jnp.stack(v_p), jnp.stack(k_s), jnp.stack(v_s),
            jnp.stack(conv_p), jnp.stack(conv_s),
            jnp.stack([a.reshape(nd, dseq, d) for a in chunk_v]))
```

```python
import functools

import jax
import jax.numpy as jnp
from jax import lax
from jax.experimental import pallas as pl
from jax.experimental.pallas import tpu as pltpu

F32 = jnp.float32
BF16 = jnp.bfloat16

DEPTH = 4
N_MIXERS = 3
CHUNK = 128
A_GROUPS = 4
CONV_W = 3
N_HEADS = 16
ALPHA = (2 * DEPTH) ** 0.25
LN_EPS = 1e-5

LANES = 128
SUBLANES = 8
ROW_TILE = 512
MIXER_ROW_TILE = 1024
SUB_TILE = 512
ATTN_TQ = 1024
ATTN_TK = 256
ATTN_UNROLL = 2
LOG2E = 1.4426950408889634
SAMPLE_PAGES = 16
NEW_PAD = 16
VMEM_LIMIT = 56 * 1024 * 1024


def _params(*semantics):
    return pltpu.CompilerParams(dimension_semantics=semantics, vmem_limit_bytes=VMEM_LIMIT)


def _resident(shape):
    zeros = (0,) * len(shape)
    return pl.BlockSpec(shape, lambda *_: zeros, pipeline_mode=pl.Buffered(1))


def _rows(tm, width):
    return pl.BlockSpec((tm, width), lambda i: (i, 0))


def _dot(a, b):
    return jnp.dot(a, b, preferred_element_type=F32)


def _layer_norm(x, g, b):
    mu = jnp.mean(x, axis=-1, keepdims=True)
    xc = x - mu
    var = jnp.mean(xc * xc, axis=-1, keepdims=True)
    return xc * lax.rsqrt(var + LN_EPS) * g + b


def _sigmoid(x):
    return 1.0 / (1.0 + jnp.exp(-x))


def _sub_tiles(tm):
    sub = min(tm, SUB_TILE)
    assert tm % sub == 0
    return [slice(r0, r0 + sub) for r0 in range(0, tm, sub)]


def _run_staggered(stages, tiles):
    for step in range(len(stages) + len(tiles) - 1):
        for lag, rows in enumerate(tiles):
            if 0 <= step - lag < len(stages):
                stages[step - lag](rows)


def _chunk_mixer_body(x_ref, w_in_ref, lng_ref, lnb_ref, ws_ref, bmap_ref, w_out_ref, mg_ref, mb_ref,
                      out_ref, *rest, emit_v):
    if emit_v:
        v_out_ref, u_ref, v_ref, s_ref = rest
    else:
        u_ref, v_ref, s_ref = rest
    tm, d = x_ref.shape
    gd = d // A_GROUPS

    def v_matmul(rows):
        s_ref[rows, :] = _dot(x_ref[rows, :].astype(BF16), w_in_ref[:, d:])

    def v_norm(rows):
        v = _layer_norm(jax.nn.gelu(s_ref[rows, :]), lng_ref[...], lnb_ref[...])
        if emit_v:
            v_out_ref[rows, :] = v
        v_ref[rows, :] = v.astype(BF16)

    def u_matmul(rows):
        u_ref[rows, :] = _dot(x_ref[rows, :].astype(BF16), w_in_ref[:, :d])

    def u_gelu(rows):
        u_ref[rows, :] = jax.nn.gelu(u_ref[rows, :])

    def mixing(rows):
        for r0 in range(rows.start, rows.stop, CHUNK):
            for g in range(A_GROUPS):
                c0 = g * gd
                s_ref[r0:r0 + CHUNK, c0:c0 + gd] = (
                    _dot(ws_ref[g], v_ref[r0:r0 + CHUNK, c0:c0 + gd]) + bmap_ref[:, c0:c0 + gd])

    def out_matmul(rows):
        gated = (u_ref[rows, :] * s_ref[rows, :]).astype(BF16)
        s_ref[rows, :] = _dot(gated, w_out_ref[...])

    def out_norm(rows):
        out_ref[rows, :] = _layer_norm(ALPHA * x_ref[rows, :] + s_ref[rows, :], mg_ref[...], mb_ref[...])

    _run_staggered((v_matmul, v_norm, u_matmul, u_gelu, mixing, out_matmul, out_norm), _sub_tiles(tm))


def _chunk_mixer(x, w_in, ln_g, ln_b, w_s, bmap, w_out, mg, mb, *, tm, emit_v):
    t, d = x.shape
    out_shape = [jax.ShapeDtypeStruct((t, d), F32)]
    out_specs = [_rows(tm, d)]
    if emit_v:
        out_shape.append(jax.ShapeDtypeStruct((t, d), F32))
        out_specs.append(_rows(tm, d))
    res = pl.pallas_call(
        functools.partial(_chunk_mixer_body, emit_v=emit_v),
        grid=(t // tm,),
        in_specs=[_rows(tm, d), _resident(w_in.shape), _resident(ln_g.shape), _resident(ln_b.shape),
                  _resident(w_s.shape), _resident(bmap.shape), _resident(w_out.shape),
                  _resident(mg.shape), _resident(mb.shape)],
        out_specs=out_specs,
        out_shape=out_shape,
        scratch_shapes=[pltpu.VMEM((tm, d), F32), pltpu.VMEM((tm, d), BF16), pltpu.VMEM((tm, d), F32)],
        compiler_params=_params("arbitrary"),
        name="chunk_mixer",
    )(x, w_in, ln_g, ln_b, w_s, bmap, w_out, mg, mb)
    return res if emit_v else (res[0], None)


def _conv_project(x_ref, w_in_ref, buf_ref):
    tm, d = x_ref.shape
    xb = x_ref[...].astype(BF16)
    b_gate = _dot(xb, w_in_ref[:, :d])
    xc = _dot(xb, w_in_ref[:, d:2 * d]) * _dot(xb, w_in_ref[:, 2 * d:])
    buf_ref[SUBLANES:SUBLANES + tm, :] = xc
    return b_gate, xc


def _conv_finish(x_ref, b_gate, y, w_out_ref, mg_ref, mb_ref, out_ref):
    mix = _dot((b_gate * y).astype(BF16), w_out_ref[...])
    out_ref[...] = _layer_norm(ALPHA * x_ref[...] + mix, mg_ref[...], mb_ref[...])


def _conv_prompt_body(x_ref, w_in_ref, cw_ref, w_out_ref, mg_ref, mb_ref, out_ref, tail_ref, buf_ref,
                      gate_ref, *, tiles_per_seq):
    tm, d = x_ref.shape
    @pl.when(pl.program_id(0) % tiles_per_seq == 0)
    def _():
        buf_ref[0:SUBLANES, :] = jnp.zeros((SUBLANES, d), F32)

    def conv_input(rows, back):
        return buf_ref[SUBLANES - back + rows.start:SUBLANES - back + rows.stop, :]

    def project(rows):
        xb = x_ref[rows, :].astype(BF16)
        gate_ref[rows, :] = _dot(xb, w_in_ref[:, :d])
        buf_ref[SUBLANES + rows.start:SUBLANES + rows.stop, :] = (
            _dot(xb, w_in_ref[:, d:2 * d]) * _dot(xb, w_in_ref[:, 2 * d:]))

    def out_matmul(rows):
        y = (cw_ref[0:1, :] * conv_input(rows, 2) + cw_ref[1:2, :] * conv_input(rows, 1)
             + cw_ref[2:3, :] * conv_input(rows, 0))
        gate_ref[rows, :] = _dot((gate_ref[rows, :] * y).astype(BF16), w_out_ref[...])

    def out_norm(rows):
        out_ref[rows, :] = _layer_norm(ALPHA * x_ref[rows, :] + gate_ref[rows, :], mg_ref[...],
                                       mb_ref[...])

    _run_staggered((project, out_matmul, out_norm), _sub_tiles(tm))
    tail = buf_ref[tm:tm + SUBLANES, :]
    tail_ref[0] = tail
    buf_ref[0:SUBLANES, :] = tail


def _conv_mixer_prompt(x, w_in, cw, w_out, mg, mb, *, tm, seq):
    t, d = x.shape
    tiles_per_seq = seq // tm
    return pl.pallas_call(
        functools.partial(_conv_prompt_body, tiles_per_seq=tiles_per_seq),
        grid=(t // tm,),
        in_specs=[_rows(tm, d), _resident(w_in.shape), _resident(cw.shape), _resident(w_out.shape),
                  _resident(mg.shape), _resident(mb.shape)],
        out_specs=[_rows(tm, d), pl.BlockSpec((1, SUBLANES, d), lambda i: (i // tiles_per_seq, 0, 0))],
        out_shape=[jax.ShapeDtypeStruct((t, d), F32),
                   jax.ShapeDtypeStruct((t // seq, SUBLANES, d), F32)],
        scratch_shapes=[pltpu.VMEM((tm + SUBLANES, d), F32), pltpu.VMEM((tm, d), F32)],
        compiler_params=_params("arbitrary"),
        name="conv_mixer_prompt",
    )(x, w_in, cw, w_out, mg, mb)


def _conv_sample_body(x_ref, s1_ref, s2_ref, w_in_ref, cw_ref, w_out_ref, mg_ref, mb_ref,
                      out_ref, xc_ref, buf_ref, *, seq):
    tm, d = x_ref.shape
    buf_ref[0:SUBLANES, :] = jnp.zeros((SUBLANES, d), F32)
    b_gate, xc = _conv_project(x_ref, w_in_ref, buf_ref)
    xc_ref[...] = xc
    step = lax.rem(lax.broadcasted_iota(jnp.int32, (tm, 1), 0), seq)
    prev1 = jnp.where(step >= 1, buf_ref[SUBLANES - 1:SUBLANES - 1 + tm, :], s1_ref[...])
    prev2 = jnp.where(step >= 2, buf_ref[SUBLANES - 2:SUBLANES - 2 + tm, :], s2_ref[...])
    y = cw_ref[0:1, :] * prev2 + cw_ref[1:2, :] * prev1 + cw_ref[2:3, :] * xc
    _conv_finish(x_ref, b_gate, y, w_out_ref, mg_ref, mb_ref, out_ref)


def _conv_mixer_sample(x, s1, s2, w_in, cw, w_out, mg, mb, *, seq):
    t, d = x.shape
    args = (x, s1, s2, w_in, cw, w_out, mg, mb)
    return pl.pallas_call(
        functools.partial(_conv_sample_body, seq=seq),
        grid=(1,),
        in_specs=[_resident(a.shape) for a in args],
        out_specs=[_resident((t, d)), _resident((t, d))],
        out_shape=[jax.ShapeDtypeStruct((t, d), F32), jax.ShapeDtypeStruct((t, d), F32)],
        scratch_shapes=[pltpu.VMEM((t + SUBLANES, d), F32)],
        compiler_params=_params("arbitrary"),
        name="conv_mixer_sample",
    )(*args)


def _qkv_body(x_ref, w_ref, k_ref, v_ref, qb_ref, kb_ref, vb_ref, *, scale, transposed):
    d = x_ref.shape[1]
    xb = x_ref[...].astype(BF16)
    qb_ref[...] = (_dot(xb, w_ref[:, :d]) * scale).astype(BF16)
    k = _dot(xb, w_ref[:, d:2 * d])
    kb_ref[...] = k.astype(BF16)
    v = _dot(xb, w_ref[:, 2 * d:])
    vb_ref[...] = v.astype(BF16)
    if transposed:
        k_ref[0] = k.T
        v_ref[0] = v.T
    else:
        k_ref[...] = k
        v_ref[...] = v


def _qkv_project(x, w_qkv, *, tm, seq=None):
    t, d = x.shape
    scale = (d // N_HEADS) ** -0.5
    if seq is None:
        kv_shape, kv_spec = jax.ShapeDtypeStruct((t, d), F32), _rows(tm, d)
    else:
        tiles_per_seq = seq // tm
        kv_shape = jax.ShapeDtypeStruct((t // seq, d, seq), F32)
        kv_spec = pl.BlockSpec((1, d, tm), lambda i: (i // tiles_per_seq, 0, i % tiles_per_seq))
    return pl.pallas_call(
        functools.partial(_qkv_body, scale=scale, transposed=seq is not None),
        grid=(t // tm,),
        in_specs=[_rows(tm, d), _resident(w_qkv.shape)],
        out_specs=[kv_spec] * 2 + [_rows(tm, d)] * 3,
        out_shape=[kv_shape] * 2 + [jax.ShapeDtypeStruct((t, d), BF16)] * 3,
        compiler_params=_params("arbitrary"),
        name="qkv_project",
    )(x, w_qkv)


_CONTRACT_LAST = (((1,), (1,)), ((), ()))


def _suffix_ones(tk):
    row = lax.broadcasted_iota(jnp.int32, (tk, tk), 0)
    col = lax.broadcasted_iota(jnp.int32, (tk, tk), 1)
    return jnp.where(row > col, 1.0, 0.0).astype(BF16)


def _stick_scores(z, u, mask):
    sp = jnp.maximum(z, 0.0) + jnp.log(1.0 + jnp.exp2(jnp.abs(z) * -LOG2E))
    log_beta = z - sp
    if mask is not None:
        sp = mask(sp)
    suffix = _dot(sp.astype(BF16), u)
    return log_beta - suffix, suffix[:, 0:1] + sp[:, 0:1]


def _stick_weights(exponent, newer, mask):
    att = jnp.exp(exponent - newer)
    if mask is not None:
        att = mask(att)
    return att.astype(BF16)


def _attn_prompt_body(bias_ref, q_ref, k_ref, v_ref, o_ref, newer_ref, acc_ref, *, tq, tk, unroll):
    pair = pl.program_id(1)
    qi = pl.program_id(2)
    half = LANES // 2
    low = lax.broadcasted_iota(jnp.int32, (1, LANES), 1) < half
    q = q_ref[0]
    zero = jnp.zeros_like(q)
    qs = jnp.concatenate([jnp.where(low, q, zero), jnp.where(low, zero, q)], axis=0)
    row = lax.broadcasted_iota(jnp.int32, (2 * tq, 1), 0)
    first = row < tq
    bias = jnp.where(first, bias_ref[2 * pair], bias_ref[2 * pair + 1])
    u_mat = _suffix_ones(tk)

    def from_row(x, lo):
        return x if lo == 0 else jnp.concatenate([x[lo:tq], x[tq + lo:]], axis=0)

    def logits(j, lo=0):
        k0 = pl.multiple_of(j * tk, tk)
        return lax.dot_general(from_row(qs, lo), k_ref[0, pl.ds(k0, tk), :], _CONTRACT_LAST,
                               preferred_element_type=F32) + from_row(bias, lo)

    def accumulate(j, exponent, newer, mask, lo=0):
        k0 = pl.multiple_of(j * tk, tk)
        n = tq - lo
        ab = _stick_weights(exponent, newer, mask)
        vb = v_ref[0, pl.ds(k0, tk), :]
        vzero = jnp.zeros_like(vb)
        a2 = jnp.concatenate([ab[:n], ab[n:]], axis=1)
        v2 = jnp.concatenate([jnp.where(low, vb, vzero), jnp.where(low, vzero, vb)], axis=0)
        acc_ref[lo:tq, :] += _dot(a2, v2)

    newer_ref[...] = jnp.zeros_like(newer_ref)
    acc_ref[...] = jnp.zeros_like(acc_ref)
    per_tile = tq // tk
    tri = (lax.broadcasted_iota(jnp.int32, (tk, tk), 1) < lax.broadcasted_iota(jnp.int32, (tk, tk), 0))
    halves = []
    for g in reversed(range(per_tile)):
        lo = g * tk
        j = qi * per_tile + g
        n = tq - lo

        def mask(x, n=n):
            pieces = []
            for r0 in (0, n):
                pieces += [jnp.where(tri, x[r0:r0 + tk], 0.0), x[r0 + tk:r0 + n]]
            return jnp.concatenate([p for p in pieces if p.shape[0]], axis=0)

        halves.append((j, lo, mask) + _stick_scores(logits(j, lo), u_mat, mask))
    for j, lo, mask, exponent, total in halves:
        newer = jnp.concatenate([newer_ref[lo:tq, :], newer_ref[tq + lo:, :]], axis=0)
        accumulate(j, exponent, newer, mask, lo)
        newer_ref[lo:tq, :] += total[:tq - lo]
        newer_ref[tq + lo:, :] += total[tq - lo:]

    def trip(i, _):
        j_newest = qi * per_tile - 1 - unroll * i
        halves = [_stick_scores(logits(j_newest - n), u_mat, None) for n in range(unroll)]
        newer = newer_ref[...]
        for n, (exponent, total) in enumerate(halves):
            accumulate(j_newest - n, exponent, newer, None)
            newer = newer + total
        newer_ref[...] = newer
        return 0

    lax.fori_loop(0, qi * (per_tile // unroll), trip, 0)
    o_ref[0] = acc_ref[...].astype(o_ref.dtype)


def _attn_prompt(bias, qb, kb, vb, *, tq, tk, unroll):
    nb, seq, d = qb.shape
    assert seq % tq == 0 and tq % tk == 0 and (tq // tk) % unroll == 0
    q_spec = pl.BlockSpec((1, tq, LANES), lambda b, p, i: (b, i, p))
    kv_spec = pl.BlockSpec((1, seq, LANES), lambda b, p, i: (b, 0, p))
    return pl.pallas_call(
        functools.partial(_attn_prompt_body, tq=tq, tk=tk, unroll=unroll),
        grid=(nb, d // LANES, seq // tq),
        in_specs=[pl.BlockSpec(memory_space=pltpu.SMEM), q_spec, kv_spec, kv_spec],
        out_specs=q_spec,
        out_shape=jax.ShapeDtypeStruct((nb, seq, d), BF16),
        scratch_shapes=[pltpu.VMEM((2 * tq, 1), F32), pltpu.VMEM((tq, LANES), F32)],
        compiler_params=_params("arbitrary", "arbitrary", "arbitrary"),
        name="attn_prompt",
    )(bias, qb, kb, vb)


def _attn_sample_body(pt_ref, qbd_ref, bias_ref, hm_ref, sel_ref, kn_ref, vn_ref, *rest, dec_seq, pages):
    kt_refs, vt_refs = rest[:pages], rest[pages:2 * pages]
    o_ref, newer_ref, acc_ref = rest[2 * pages:]
    j = pl.program_id(1)
    qbd = qbd_ref[0]
    m = qbd.shape[0]
    bias = bias_ref[...]

    @pl.when(j == 0)
    def _():
        step = lax.rem(lax.broadcasted_iota(jnp.int32, (m, 1), 0), dec_seq)
        visible = lax.broadcasted_iota(jnp.int32, (1, NEW_PAD), 1) < step
        mask = lambda x: jnp.where(visible, x, 0.0)
        z = lax.dot_general(qbd, kn_ref[0].astype(BF16), _CONTRACT_LAST,
                            preferred_element_type=F32) + bias
        exponent, total = _stick_scores(z, _suffix_ones(NEW_PAD), mask)
        ab = _stick_weights(exponent, jnp.zeros((m, 1), F32), mask)
        acc_ref[...] = _dot(ab, vn_ref[0].astype(BF16))
        newer_ref[...] = total

    def two_pages(refs, b):
        return jnp.concatenate([refs[2 * b + 1][0].astype(BF16), refs[2 * b][0].astype(BF16)], axis=1)

    u = _suffix_ones(2 * kt_refs[0].shape[2])
    blocks = [_stick_scores(_dot(qbd, two_pages(kt_refs, b)) + bias, u, None)
              for b in range(pages // 2)]
    newer = newer_ref[...]
    for b, (exponent, total) in enumerate(blocks):
        ab = _stick_weights(exponent, newer, None)
        acc_ref[...] += lax.dot_general(ab, two_pages(vt_refs, b), _CONTRACT_LAST,
                                        preferred_element_type=F32)
        newer = newer + total
    newer_ref[...] = newer

    @pl.when(j == pl.num_programs(1) - 1)
    def _():
        own = (acc_ref[...] * hm_ref[...]).astype(BF16)
        o_ref[0] = _dot(sel_ref[...], own).astype(o_ref.dtype)


def _attn_sample(page_table, qbd, bias_rows, hm_rows, sel, k_new, v_new, cache_kt, cache_vt, *, dec_seq,
                 pages):
    nseq, n_pages = page_table.shape
    _, m, d = qbd.shape
    page = cache_kt.shape[2]
    assert pages % 2 == 0 and n_pages % pages == 0
    pt_flat = page_table.reshape(-1)

    def page_spec(u):
        return pl.BlockSpec(
            (1, d, page), lambda b, j, pt: (pt[b * n_pages + n_pages - 1 - (j * pages + u)], 0, 0))

    per_seq = lambda b, j, pt: (b, 0, 0)
    const2 = lambda b, j, pt: (0, 0)
    grid_spec = pltpu.PrefetchScalarGridSpec(
        num_scalar_prefetch=1,
        grid=(nseq, n_pages // pages),
        in_specs=[pl.BlockSpec((1, m, d), per_seq),
                  pl.BlockSpec(bias_rows.shape, const2),
                  pl.BlockSpec(hm_rows.shape, const2),
                  pl.BlockSpec(sel.shape, const2),
                  pl.BlockSpec((1, NEW_PAD, d), per_seq),
                  pl.BlockSpec((1, NEW_PAD, d), per_seq)]
                 + [page_spec(u) for u in range(pages)] * 2,
        out_specs=pl.BlockSpec((1, NEW_PAD, d), per_seq),
        scratch_shapes=[pltpu.VMEM((m, 1), F32), pltpu.VMEM((m, d), F32)],
    )
    return pl.pallas_call(
        functools.partial(_attn_sample_body, dec_seq=dec_seq, pages=pages),
        grid_spec=grid_spec,
        out_shape=jax.ShapeDtypeStruct((nseq, NEW_PAD, d), BF16),
        compiler_params=_params("arbitrary", "arbitrary"),
        name="attn_sample",
    )(pt_flat, qbd, bias_rows, hm_rows, sel, k_new, v_new, *([cache_kt] * pages), *([cache_vt] * pages))


def _proj_ln_body(x_ref, o_ref, w_ref, g_ref, b_ref, out_ref, mix_ref):
    def matmul(rows):
        mix_ref[rows, :] = _dot(o_ref[rows, :], w_ref[...])

    def norm(rows):
        out_ref[rows, :] = _layer_norm(ALPHA * x_ref[rows, :] + mix_ref[rows, :], g_ref[...], b_ref[...])

    _run_staggered((matmul, norm), _sub_tiles(x_ref.shape[0]))


def _proj_ln(x, o, w_out, mg, mb, *, tm):
    t, d = x.shape
    return pl.pallas_call(
        _proj_ln_body,
        grid=(t // tm,),
        in_specs=[_rows(tm, d), _rows(tm, d), _resident(w_out.shape), _resident(mg.shape),
                  _resident(mb.shape)],
        out_specs=_rows(tm, d),
        out_shape=jax.ShapeDtypeStruct((t, d), F32),
        scratch_shapes=[pltpu.VMEM((tm, d), F32)],
        compiler_params=_params("arbitrary"),
        name="attn_out_proj",
    )(x, o, w_out, mg, mb)


def _ff_chunks(d_ff, width):
    return tuple((c0, min(c0 + width, d_ff)) for c0 in range(0, d_ff, width))


def _ffn_body(x_ref, p_ref, w_up_ref, w_down_ref, g_ref, b_ref, wg_ref, wp_ref, out_ref, acc_ref):
    d_ff = w_down_ref.shape[0]
    x = x_ref[...]
    xb = x.astype(BF16)
    for idx, (c0, c1) in enumerate(_ff_chunks(d_ff, x.shape[1])):
        gate = _dot(xb, w_up_ref[:, c0:c1])
        up = _dot(xb, w_up_ref[:, d_ff + c0:d_ff + c1])
        hidden = (gate * _sigmoid(gate) * up).astype(BF16)
        part = _dot(hidden, w_down_ref[c0:c1, :])
        if idx == 0:
            acc_ref[...] = part
        else:
            acc_ref[...] += part
    x2 = _layer_norm(ALPHA * x + acc_ref[...], g_ref[...], b_ref[...])
    ple_gate = _sigmoid(_dot(x2.astype(BF16), wg_ref[...]))
    out_ref[...] = x2 + ple_gate * _dot(p_ref[...].astype(BF16), wp_ref[...])


def _ffn_ple(x, p, w_up, w_down, g, b, w_gate, w_proj, *, tm):
    t, d = x.shape
    return pl.pallas_call(
        _ffn_body,
        grid=(t // tm,),
        in_specs=[_rows(tm, d), _rows(tm, p.shape[1]), _resident(w_up.shape), _resident(w_down.shape),
                  _resident(g.shape), _resident(b.shape), _resident(w_gate.shape),
                  _resident(w_proj.shape)],
        out_specs=_rows(tm, d),
        out_shape=jax.ShapeDtypeStruct((t, d), F32),
        scratch_shapes=[pltpu.VMEM((tm, d), F32)],
        compiler_params=_params("arbitrary"),
        name="ffn_ple",
    )(x, p, w_up, w_down, g, b, w_gate, w_proj)


def _row(vec):
    return vec.reshape(1, -1).astype(F32)


def _head_mask(d):
    head_of_lane = jnp.arange(d, dtype=jnp.int32) // (d // N_HEADS)
    return (head_of_lane[None, :] == jnp.arange(N_HEADS, dtype=jnp.int32)[:, None])


def _trunk(x, p, W, *, nseq, seq, tm, sample):
    t, d = x.shape
    mixer_tm = tm if sample is not None else min(MIXER_ROW_TILE, seq)
    chunk_v, conv_new, k_new, v_new = [], [], [], []
    for i in range(DEPTH):
        j, kind = divmod(i, N_MIXERS)
        mg, mb = _row(W["ln_mix_g"][i]), _row(W["ln_mix_b"][i])
        if kind == 0:
            w_s = jnp.where(jnp.tril(jnp.ones((CHUNK, CHUNK), bool))[None], W["a_w_s"][j], 0.0)
            b_s = W["a_b_s"][j]
            if sample is None:
                w_mix, b_rows = w_s, b_s.T
            else:
                eye = jnp.eye(t // seq, dtype=F32)
                w_mix = jax.vmap(lambda w: jnp.kron(eye, w[:seq, :seq]))(w_s)
                b_rows = jnp.tile(b_s[:, :seq].T, (t // seq, 1))
            bmap = jnp.repeat(b_rows.astype(F32), d // A_GROUPS, axis=1)
            x, v_rows = _chunk_mixer(
                x, W["a_w_in"][j].astype(BF16), _row(W["a_ln_g"][j]), _row(W["a_ln_b"][j]),
                w_mix.astype(BF16), bmap, W["a_w_out"][j].astype(BF16), mg, mb,
                tm=mixer_tm, emit_v=sample is not None)
            chunk_v.append(v_rows)
        elif kind == 1:
            w_in, cw, w_out = W["b_w_in"][j].astype(BF16), W["b_conv"][j], W["b_w_out"][j].astype(BF16)
            if sample is None:
                x, tail = _conv_mixer_prompt(x, w_in, cw, w_out, mg, mb, tm=mixer_tm, seq=seq)
                conv_new.append(tail[:, SUBLANES - (CONV_W - 1):])
            else:
                st = sample["state_conv"][j]
                zeros = jnp.zeros((nseq, seq, d), F32)
                s1 = zeros.at[:, 0].set(st[:, 1]).reshape(t, d)
                s2 = zeros.at[:, 0].set(st[:, 0]).at[:, 1].set(st[:, 1]).reshape(t, d)
                x, xc = _conv_mixer_sample(x, s1, s2, w_in, cw, w_out, mg, mb, seq=seq)
                conv_new.append(xc.reshape(nseq, seq, d)[:, seq - (CONV_W - 1):])
        else:
            heads = (N_HEADS, d // N_HEADS)
            k, v, qb, kb, vb = _qkv_project(x, W["c_w_qkv"][j].astype(BF16), tm=tm,
                                            seq=seq if sample is None else None)
            bias = W["c_b_beta"][j].astype(F32)
            if sample is None:
                k_new.append(jnp.transpose(k.reshape((nseq,) + heads + (seq,)), (0, 3, 1, 2)))
                v_new.append(jnp.transpose(v.reshape((nseq,) + heads + (seq,)), (0, 3, 1, 2)))
                o = _attn_prompt(bias, qb.reshape(nseq, seq, d), kb.reshape(nseq, seq, d),
                                 vb.reshape(nseq, seq, d), tq=ATTN_TQ, tk=ATTN_TK,
                                 unroll=ATTN_UNROLL).reshape(t, d)
            else:
                hm = _head_mask(d)
                qbd = jnp.where(hm[None, :, None, :], qb.reshape(nseq, 1, seq, d), 0)
                qbd = qbd.reshape(nseq, N_HEADS * seq, d)
                bias_rows = jnp.repeat(bias, seq)[:, None]
                hm_rows = jnp.repeat(hm.astype(F32), seq, axis=0)
                sel = (jnp.arange(NEW_PAD)[:, None] == (jnp.arange(N_HEADS * seq) % seq)[None, :])
                pad = ((0, 0), (0, NEW_PAD - seq), (0, 0))
                n_pool, page = sample["cache_k"].shape[1:3]
                transposed = lambda c: jnp.transpose(c[j], (0, 2, 3, 1)).reshape(n_pool, d, page)
                o = _attn_sample(
                    sample["page_table"], qbd, bias_rows, hm_rows, sel.astype(BF16),
                    jnp.pad(k.reshape(nseq, seq, d), pad), jnp.pad(v.reshape(nseq, seq, d), pad),
                    transposed(sample["cache_k"]), transposed(sample["cache_v"]), dec_seq=seq,
                    pages=SAMPLE_PAGES)
                o = o[:, :seq].reshape(t, d)
                k_new.append(k.reshape((nseq, seq) + heads))
                v_new.append(v.reshape((nseq, seq) + heads))
            x = _proj_ln(x, o, W["c_w_out"][j].astype(BF16), mg, mb, tm=mixer_tm)
        x = _ffn_ple(x, p[i], W["f_w_up"][i].astype(BF16), W["f_w_down"][i].astype(BF16),
                     _row(W["ln_ffn_g"][i]), _row(W["ln_ffn_b"][i]),
                     W["ple_w_gate"][i].astype(BF16), W["ple_w_proj"][i].astype(BF16), tm=tm)
    return x, chunk_v, conv_new, k_new, v_new


def kernel(x_prompt, x_sample, cache_k, cache_v, state_conv, page_table, p_prompt, p_sample, a_w_in, a_ln_g, a_ln_b, a_w_s, a_b_s, a_w_out, b_w_in, b_conv, b_w_out, c_w_qkv, c_b_beta, c_w_out, ln_mix_g, ln_mix_b, ln_ffn_g, ln_ffn_b, f_w_up, f_w_down, ple_w_gate, ple_w_proj):
    W = dict(a_w_in=a_w_in, a_ln_g=a_ln_g, a_ln_b=a_ln_b, a_w_s=a_w_s, a_b_s=a_b_s, a_w_out=a_w_out,
             b_w_in=b_w_in, b_conv=b_conv, b_w_out=b_w_out, c_w_qkv=c_w_qkv, c_b_beta=c_b_beta,
             c_w_out=c_w_out, ln_mix_g=ln_mix_g, ln_mix_b=ln_mix_b, ln_ffn_g=ln_ffn_g, ln_ffn_b=ln_ffn_b,
             f_w_up=f_w_up, f_w_down=f_w_down, ple_w_gate=ple_w_gate, ple_w_proj=ple_w_proj)
    nb, seq, d = x_prompt.shape
    nd, dseq, _ = x_sample.shape

    yp, _, conv_p, k_p, v_p = _trunk(
        x_prompt.reshape(nb * seq, d), p_prompt.reshape(DEPTH, nb * seq, -1), W,
        nseq=nb, seq=seq, tm=min(ROW_TILE, seq), sample=None)
    state = dict(state_conv=state_conv, cache_k=cache_k, cache_v=cache_v, page_table=page_table)
    ys, chunk_v, conv_s, k_s, v_s = _trunk(
        x_sample.reshape(nd * dseq, d), p_sample.reshape(DEPTH, nd * dseq, -1), W,
        nseq=nd, seq=dseq, tm=nd * dseq, sample=state)

    return (yp.reshape(nb, seq, d), ys.reshape(nd, dseq, d),
            jnp.stack(k_p), jnp.stack(v_p), jnp.stack(k_s), jnp.stack(v_s),
            jnp.stack(conv_p), jnp.stack(conv_s),
            jnp.stack([a.reshape(nd, dseq, d) for a in chunk_v]))
```
